```python
import math
import jax, jax.numpy as jnp
from jax import lax
import numpy as np

D_MODEL = 4096
BATCH = 1
SEQ = 8192
DEPTH = 2

N_MEM = 256
CONV_DIM = 2048
CONV_K = 3
N_HEADS = 16
N_KV_HEADS = 4
HEAD_DIM = 128
GROUP = N_HEADS // N_KV_HEADS
ATTN_DIM = N_HEADS * HEAD_DIM
KV_DIM = N_KV_HEADS * HEAD_DIM
IDX_HEADS = 32
IDX_DIM = 64
TOPK_MAX = 256
Q_BLOCK = 128
N_BUCKETS = 32
MAX_DISTANCE = 128
X_HEADS = 4
X_HEAD_DIM = 128
X_DIM = X_HEADS * X_HEAD_DIM
D_FF = 4 * D_MODEL
EPS = 1e-6

SPLITS = (CONV_DIM, CONV_DIM, CONV_DIM,
          ATTN_DIM, KV_DIM, KV_DIM,
          IDX_HEADS * IDX_DIM, IDX_DIM, IDX_HEADS,
          D_MODEL, D_MODEL)
IN_DIM = int(sum(SPLITS))
SPLIT_POINTS = tuple(int(s) for s in np.cumsum(SPLITS)[:-1])

kernel_name = "hybrid_shortconv_dsa_gated_trunk"


def rmsnorm(x, g):
    xf = x.astype(jnp.float32)
    inv = lax.rsqrt(jnp.mean(xf * xf, axis=-1, keepdims=True) + EPS)
    return (xf * inv).astype(x.dtype) * g


def t5_bucket(dist):
    n = jnp.maximum(dist, 0)
    max_exact = N_BUCKETS // 2
    nf = jnp.maximum(n, 1).astype(jnp.float32)
    large = max_exact + (jnp.log(nf / max_exact) / math.log(MAX_DISTANCE / max_exact)
                         * (N_BUCKETS - max_exact)).astype(jnp.int32)
    large = jnp.minimum(large, N_BUCKETS - 1)
    return jnp.where(n < max_exact, n, large)


def causal_dwconv(u, w):
    S = u.shape[1]
    up = jnp.pad(u, ((0, 0), (CONV_K - 1, 0), (0, 0)))
    z = w[CONV_K - 1] * u
    for j in range(CONV_K - 1):
        z = z + w[j] * up[:, j:j + S]
    return z


def dsa_attention(q, k, v, iq, ik, iw, rel_bias):
    B, S = q.shape[0], q.shape[1]
    topk = min(TOPK_MAX, S // 4)
    n_blocks = S // Q_BLOCK
    kpos = jnp.arange(S, dtype=jnp.int32)
    idx_scale = IDX_DIM ** -0.5
    attn_scale = HEAD_DIM ** -0.5
    gather = jax.vmap(lambda arr, ii: arr[ii])

    def block(i):
        start = i * Q_BLOCK
        qb = lax.dynamic_slice_in_dim(q, start, Q_BLOCK, axis=1)
        iqb = lax.dynamic_slice_in_dim(iq, start, Q_BLOCK, axis=1)
        iwb = lax.dynamic_slice_in_dim(iw, start, Q_BLOCK, axis=1)
        qpos = start + jnp.arange(Q_BLOCK, dtype=jnp.int32)
        causal = kpos[None, :] <= qpos[:, None]
        sc = jax.nn.relu(jnp.einsum('bqhd,bsd->bqhs', iqb, ik) * idx_scale)
        sc = jnp.einsum('bqh,bqhs->bqs', iwb, sc).astype(jnp.float32)
        sc = jnp.where(causal[None], sc, -jnp.inf)
        _, sel = lax.top_k(sc, topk)
        k_sel = gather(k, sel)
        v_sel = gather(v, sel)
        qg = qb.reshape(B, Q_BLOCK, N_KV_HEADS, GROUP, HEAD_DIM)
        logits = jnp.einsum('bqhgd,bqshd->bqhgs', qg, k_sel).astype(jnp.float32) * attn_scale
        dist = qpos[None, :, None] - sel
        bias = rel_bias[t5_bucket(dist)].astype(jnp.float32)
        bias = bias.reshape(B, Q_BLOCK, topk, N_KV_HEADS, GROUP).transpose(0, 1, 3, 4, 2)
        valid = (dist >= 0)[:, :, None, None, :]
        logits = jnp.where(valid, logits + bias, -jnp.inf)
        p = jax.nn.softmax(logits, axis=-1).astype(v.dtype)
        o = jnp.einsum('bqhgs,bqshd->bqhgd', p, v_sel)
        return o.reshape(B, Q_BLOCK, ATTN_DIM)

    out = lax.map(block, jnp.arange(n_blocks, dtype=jnp.int32))
    return out.transpose(1, 0, 2, 3).reshape(B, S, ATTN_DIM)


def memory_cross_attention(h, m, w_xq, w_xkv, w_xo):
    B, S = h.shape[0], h.shape[1]
    M = m.shape[1]
    qx = (h @ w_xq).reshape(B, S, X_HEADS, X_HEAD_DIM)
    kv = m @ w_xkv
    kx = kv[..., :X_DIM].reshape(B, M, X_HEADS, X_HEAD_DIM)
    vx = kv[..., X_DIM:].reshape(B, M, X_HEADS, X_HEAD_DIM)
    logits = jnp.einsum('bqhd,bmhd->bhqm', qx, kx).astype(jnp.float32) * (X_HEAD_DIM ** -0.5)
    p = jax.nn.softmax(logits, axis=-1).astype(vx.dtype)
    o = jnp.einsum('bhqm,bmhd->bqhd', p, vx).reshape(B, S, X_DIM)
    return o @ w_xo


def setup_inputs(seed: int = 0) -> dict:
    key = jax.random.key(seed)
    ks = jax.random.split(key, 20)

    def nrm(k, shape, scale):
        return jax.random.normal(k, shape, jnp.float32) * scale

    def gain(k, shape):
        return 1.0 + 0.02 * jax.random.normal(k, shape, jnp.float32)

    return {
        "x": nrm(ks[0], (BATCH, SEQ, D_MODEL), 1.0),
        "mem": nrm(ks[1], (BATCH, N_MEM, D_MODEL), 1.0),
        "rel_bias": nrm(ks[2], (N_BUCKETS, N_HEADS), 0.5),
        "norm_mix": gain(ks[3], (DEPTH, D_MODEL)),
        "w_in": nrm(ks[4], (DEPTH, D_MODEL, IN_DIM), D_MODEL ** -0.5),
        "conv_w": nrm(ks[5], (DEPTH, CONV_K, CONV_DIM), CONV_K ** -0.5),
        "w_conv_out": nrm(ks[6], (DEPTH, CONV_DIM, D_MODEL), CONV_DIM ** -0.5),
        "w_attn_out": nrm(ks[7], (DEPTH, ATTN_DIM, D_MODEL), ATTN_DIM ** -0.5),
        "w_mix_out": nrm(ks[8], (DEPTH, D_MODEL, D_MODEL), D_MODEL ** -0.5),
        "norm_xattn": gain(ks[9], (DEPTH, D_MODEL)),
        "norm_mem": gain(ks[10], (DEPTH, D_MODEL)),
        "w_xq": nrm(ks[11], (DEPTH, D_MODEL, X_DIM), D_MODEL ** -0.5),
        "w_xkv": nrm(ks[12], (DEPTH, D_MODEL, 2 * X_DIM), D_MODEL ** -0.5),
        "w_xo": nrm(ks[13], (DEPTH, X_DIM, D_MODEL), X_DIM ** -0.5),
        "norm_mlp": gain(ks[14], (DEPTH, D_MODEL)),
        "w_up": nrm(ks[15], (DEPTH, D_MODEL, D_FF), D_MODEL ** -0.5),
        "w_down": nrm(ks[16], (DEPTH, D_FF, D_MODEL), D_FF ** -0.5),
        "norm_final": gain(ks[17], (D_MODEL,)),
    }


def reference(x, mem, rel_bias, norm_mix, w_in, conv_w, w_conv_out, w_attn_out, w_mix_out,
              norm_xattn, norm_mem, w_xq, w_xkv, w_xo, norm_mlp, w_up, w_down, norm_final):
    B, S = x.shape[0], x.shape[1]
    for l in range(DEPTH):
        h = rmsnorm(x, norm_mix[l])
        z = h @ w_in[l]
        u, cb, cc, q, k, v, iq, ik, iw, ga, gb = jnp.split(z, SPLIT_POINTS, axis=-1)
        y_a = (cb * causal_dwconv(cc * u, conv_w[l])) @ w_conv_out[l]
        o = dsa_attention(
            q.reshape(B, S, N_HEADS, HEAD_DIM),
            k.reshape(B, S, N_KV_HEADS, HEAD_DIM),
            v.reshape(B, S, N_KV_HEADS, HEAD_DIM),
            iq.reshape(B, S, IDX_HEADS, IDX_DIM),
            ik,
            iw * (IDX_HEADS ** -0.5),
            rel_bias)
        y_b = o @ w_attn_out[l]
        merged = jax.nn.sigmoid(ga) * y_a + jax.nn.sigmoid(gb) * y_b
        x = x + merged @ w_mix_out[l]
        x = x + memory_cross_attention(rmsnorm(x, norm_xattn[l]), rmsnorm(mem, norm_mem[l]),
                                       w_xq[l], w_xkv[l], w_xo[l])
        h = rmsnorm(x, norm_mlp[l])
        x = x + jnp.square(jax.nn.relu(h @ w_up[l])) @ w_down[l]
    return rmsnorm(x, norm_final)
```

```python
import functools
import math

import numpy as np
import jax
import jax.numpy as jnp
from jax import lax
from jax.experimental import pallas as pl
from jax.experimental.pallas import tpu as pltpu

F32 = jnp.float32
BF16 = jnp.bfloat16

D_MODEL = 4096
CONV_DIM = 2048
CONV_K = 3
N_HEADS = 16
N_KV_HEADS = 4
HEAD_DIM = 128
GROUP = N_HEADS // N_KV_HEADS
ATTN_DIM = N_HEADS * HEAD_DIM
KV_DIM = N_KV_HEADS * HEAD_DIM
IDX_HEADS = 32
IDX_DIM = 64
TOPK_MAX = 256
N_BUCKETS = 32
MAX_DISTANCE = 128
X_HEADS = 4
X_HEAD_DIM = 128
X_DIM = X_HEADS * X_HEAD_DIM
EPS = 1e-6

V7X_VMEM_BYTES = 64 * 1024 * 1024
LANES = 128
SUBLANES = 8

Q_TILE = 128
IDX_KEYS = 256
FAR_KEYS = 512
NEG_INF = float("-inf")


def _vmem_limit(nbytes):
    return int(min(nbytes + (8 << 20), V7X_VMEM_BYTES - (6 << 20)))


def _rmsnorm_kernel(x_ref, g_ref, o_ref):
    x = x_ref[...]
    inv = lax.rsqrt(jnp.mean(x * x, axis=-1, keepdims=True) + EPS)
    o_ref[...] = ((x * inv) * g_ref[...]).astype(o_ref.dtype)


def _rmsnorm(x, g, out_dtype, tr=256):
    rows, d = x.shape
    tr = min(tr, rows)
    return pl.pallas_call(
        _rmsnorm_kernel,
        grid=(rows // tr,),
        in_specs=[pl.BlockSpec((tr, d), lambda i: (i, 0)),
                  pl.BlockSpec((1, d), lambda i: (0, 0))],
        out_specs=pl.BlockSpec((tr, d), lambda i: (i, 0)),
        out_shape=jax.ShapeDtypeStruct((rows, d), out_dtype),
        compiler_params=pltpu.CompilerParams(
            dimension_semantics=("parallel",),
            vmem_limit_bytes=_vmem_limit(2 * tr * d * (4 + jnp.dtype(out_dtype).itemsize))),
        name="rmsnorm",
    )(x, g.reshape(1, d))


def _mm_kernel(*refs, nk, epilogue):
    if epilogue == "residual":
        a_ref, b_ref, r_ref, o_ref = refs[:4]
        scratch = refs[4:]
    else:
        a_ref, b_ref, o_ref = refs[:3]
        r_ref = None
        scratch = refs[3:]

    def finish(acc):
        if epilogue == "relu2":
            acc = jnp.square(jnp.maximum(acc, 0.0))
        elif epilogue == "residual":
            acc = r_ref[...] + acc
        o_ref[...] = acc.astype(o_ref.dtype)

    d = jnp.dot(a_ref[...], b_ref[...], preferred_element_type=F32)
    if nk == 1:
        finish(d)
        return
    acc_ref, = scratch
    k = pl.program_id(2)

    @pl.when(k == 0)
    def _():
        acc_ref[...] = d

    @pl.when(jnp.logical_and(k > 0, k < nk - 1))
    def _():
        acc_ref[...] += d

    @pl.when(k == nk - 1)
    def _():
        finish(acc_ref[...] + d)


def _matmul(a, b, *, out_dtype, tm, tn, tk=None, epilogue=None, residual=None, name="matmul"):
    m, kdim = a.shape
    n = b.shape[1]
    tm, tn = min(tm, m), min(tn, n)
    tk = kdim if tk is None else tk
    nk = kdim // tk
    assert m % tm == 0 and n % tn == 0 and kdim % tk == 0
    in_specs = [pl.BlockSpec((tm, tk), lambda i, j, k: (i, k)),
                pl.BlockSpec((tk, tn), lambda i, j, k: (k, j))]
    args = [a, b]
    nbytes = 2 * (tm * tk + tk * tn) * 2 + 2 * tm * tn * jnp.dtype(out_dtype).itemsize
    if epilogue == "residual":
        in_specs.append(pl.BlockSpec((tm, tn), lambda i, j, k: (i, j)))
        args.append(residual)
        nbytes += 2 * tm * tn * 4
    scratch = []
    if nk > 1:
        scratch.append(pltpu.VMEM((tm, tn), F32))
        nbytes += tm * tn * 4
    nbytes += tm * tn * 4
    return pl.pallas_call(
        functools.partial(_mm_kernel, nk=nk, epilogue=epilogue),
        grid=(m // tm, n // tn, nk),
        in_specs=in_specs,
        out_specs=pl.BlockSpec((tm, tn), lambda i, j, k: (i, j)),
        out_shape=jax.ShapeDtypeStruct((m, n), out_dtype),
        scratch_shapes=scratch,
        compiler_params=pltpu.CompilerParams(
            dimension_semantics=("parallel", "parallel", "arbitrary"),
            vmem_limit_bytes=_vmem_limit(nbytes)),
        name=name,
    )(*args)


def _conv_kernel(u_ref, cb_ref, cc_ref, up_ref, ccp_ref, w_ref, o_ref, p_ref, *, tr):
    i = pl.program_id(0)
    p = cc_ref[...] * u_ref[...]
    prev = jnp.where(i > 0, ccp_ref[...] * up_ref[...], 0.0)
    p_ref[0:SUBLANES, :] = prev
    p_ref[SUBLANES:SUBLANES + tr, :] = p
    p1 = p_ref[SUBLANES - 1:SUBLANES - 1 + tr, :]
    p2 = p_ref[SUBLANES - 2:SUBLANES - 2 + tr, :]
    w = w_ref[...]
    z = w[2:3, :] * p + w[0:1, :] * p2 + w[1:2, :] * p1
    o_ref[...] = (cb_ref[...] * z).astype(o_ref.dtype)


def _gated_conv(za, conv_w, tr=512, tc=512):
    s = za.shape[0]
    tr = min(tr, s)
    nc = CONV_DIM // tc
    rb = tr // SUBLANES

    def prev_map(off):
        return lambda i, c: (jnp.maximum(i * rb - 1, 0), c + off)

    return pl.pallas_call(
        functools.partial(_conv_kernel, tr=tr),
        grid=(s // tr, nc),
        in_specs=[pl.BlockSpec((tr, tc), lambda i, c: (i, c)),
                  pl.BlockSpec((tr, tc), lambda i, c: (i, c + nc)),
                  pl.BlockSpec((tr, tc), lambda i, c: (i, c + 2 * nc)),
                  pl.BlockSpec((SUBLANES, tc), prev_map(0)),
                  pl.BlockSpec((SUBLANES, tc), prev_map(2 * nc)),
                  pl.BlockSpec((CONV_K, tc), lambda i, c: (0, c))],
        out_specs=pl.BlockSpec((tr, tc), lambda i, c: (i, c)),
        out_shape=jax.ShapeDtypeStruct((s, CONV_DIM), BF16),
        scratch_shapes=[pltpu.VMEM((tr + SUBLANES, tc), F32)],
        compiler_params=pltpu.CompilerParams(
            dimension_semantics=("parallel", "parallel"),
            vmem_limit_bytes=_vmem_limit(10 * tr * tc * 4)),
        name="gated_conv",
    )(za, za, za, za, za, conv_w)


def _t5_bucket_np(dist):
    n = np.maximum(dist, 0)
    max_exact = N_BUCKETS // 2
    nf = np.maximum(n, 1).astype(np.float64)
    large = max_exact + (np.log(nf / max_exact) / math.log(MAX_DISTANCE / max_exact)
                         * (N_BUCKETS - max_exact)).astype(np.int32)
    large = np.minimum(large, N_BUCKETS - 1)
    return np.where(n < max_exact, n, large).astype(np.int32)


def _near_bias(rel_bias):
    sl = np.arange(Q_TILE)[:, None]
    tl = np.arange(Q_TILE)[None, :]
    dist = np.stack([tl - sl + Q_TILE, tl - sl])
    assert _t5_bucket_np(np.array([Q_TILE]))[0] == N_BUCKETS - 1
    bucket = _t5_bucket_np(dist)
    rel = rel_bias - rel_bias[N_BUCKETS - 1][None, :]
    b = rel[bucket]
    b = b.reshape(2, Q_TILE, Q_TILE, N_KV_HEADS, GROUP)
    b = b.transpose(0, 3, 1, 4, 2)
    return b.reshape(2, N_KV_HEADS, Q_TILE, GROUP * Q_TILE).astype(F32)


def _dsa_kernel(q_ref, k_ref, vt_ref, iq_ref, ik_ref, w_ref, bias_ref, o_ref,
                sc_ref, acc_ref, m_ref, l_ref, *, topk, max_iters):
    i = pl.program_id(0)
    q0 = i * Q_TILE
    kf = float(topk)

    n_idx = (q0 + Q_TILE + IDX_KEYS - 1) // IDX_KEYS
    iq = iq_ref[...]
    w = w_ref[...]
    qpos = q0 + lax.broadcasted_iota(jnp.int32, (1, Q_TILE), 1)

    def fold8(x, op):
        return op(x.reshape(x.shape[0] // SUBLANES, SUBLANES, x.shape[1]), axis=0)

    def idx_body(j, carry):
        mx, mn = carry
        s0 = pl.multiple_of(j * IDX_KEYS, IDX_KEYS)
        ikb = ik_ref[pl.ds(s0, IDX_KEYS), :]
        r = lax.dot_general(ikb, iq, (((1,), (1,)), ((), ())), preferred_element_type=F32)
        r = jnp.maximum(r, 0.0) * w
        acc = r[:, 0:Q_TILE]
        for h in range(1, IDX_HEADS):
            acc = acc + r[:, h * Q_TILE:(h + 1) * Q_TILE]
        kpos = s0 + lax.broadcasted_iota(jnp.int32, (IDX_KEYS, 1), 0)
        causal = kpos <= qpos
        sc_ref[pl.ds(s0, IDX_KEYS), :] = jnp.where(causal, acc, NEG_INF)
        mx = jnp.maximum(mx, fold8(jnp.where(causal, acc, NEG_INF), jnp.max))
        mn = jnp.minimum(mn, fold8(jnp.where(causal, acc, -NEG_INF), jnp.min))
        return mx, mn

    mx8, mn8 = lax.fori_loop(
        0, n_idx, idx_body,
        (jnp.full((SUBLANES, Q_TILE), NEG_INF, F32), jnp.full((SUBLANES, Q_TILE), -NEG_INF, F32)))
    hi0 = jnp.max(mx8, axis=0, keepdims=True)
    lo0 = jnp.min(mn8, axis=0, keepdims=True)

    def count_ge(tau):
        def body(j, c):
            s0 = pl.multiple_of(j * IDX_KEYS, IDX_KEYS)
            blk = sc_ref[pl.ds(s0, IDX_KEYS), :]
            return c + fold8(jnp.where(blk >= tau, 1.0, 0.0), jnp.sum)
        c8 = lax.fori_loop(0, n_idx, body, jnp.zeros((SUBLANES, Q_TILE), F32))
        return jnp.sum(c8, axis=0, keepdims=True)

    def n_active(lo, hi, flo):
        mid = 0.5 * lo + 0.5 * hi
        act = jnp.logical_and(flo > kf, jnp.logical_and(mid > lo, mid < hi))
        return act, jnp.sum(jnp.where(act, 1.0, 0.0))

    flo0 = (qpos + 1).astype(F32)

    def bis_cond(st):
        _, _, _, n, it = st
        return jnp.logical_and(n > 0.0, it < max_iters)

    def bis_body(st):
        lo, hi, flo, _, it = st
        act, _ = n_active(lo, hi, flo)
        mid = 0.5 * lo + 0.5 * hi
        c = count_ge(mid)
        up = jnp.logical_and(act, c >= kf)
        dn = jnp.logical_and(act, c < kf)
        lo = jnp.where(up, mid, lo)
        flo = jnp.where(up, c, flo)
        hi = jnp.where(dn, mid, hi)
        _, n = n_active(lo, hi, flo)
        return lo, hi, flo, n, it + 1

    _, n0 = n_active(lo0, hi0, flo0)
    lo, hi, flo, _, _ = lax.while_loop(bis_cond, bis_body, (lo0, hi0, flo0, n0, jnp.int32(0)))
    c_hi = count_ge(hi)
    tau = jnp.where(jnp.logical_and(flo > kf, c_hi >= kf), hi, lo)

    scale = HEAD_DIM ** -0.5
    qg = []
    for g in range(N_KV_HEADS):
        qg.append(jnp.concatenate(
            [q_ref[:, (g * GROUP + hh) * HEAD_DIM:(g * GROUP + hh + 1) * HEAD_DIM] for hh in range(GROUP)],
            axis=0))

    m_ref[...] = jnp.full(m_ref.shape, NEG_INF, F32)
    l_ref[...] = jnp.zeros(l_ref.shape, F32)
    acc_ref[...] = jnp.zeros(acc_ref.shape, F32)

    def chunk(s0, nk, bias_idx):
        blk = sc_ref[pl.ds(s0, nk), :]
        madd = jnp.where(blk >= tau, 0.0, NEG_INF)
        madd = jnp.concatenate([madd] * GROUP, axis=1)
        for g in range(N_KV_HEADS):
            kb = k_ref[pl.ds(s0, nk), g * HEAD_DIM:(g + 1) * HEAD_DIM]
            vtb = vt_ref[g * HEAD_DIM:(g + 1) * HEAD_DIM, pl.ds(s0, nk)]
            s = lax.dot_general(kb, qg[g], (((1,), (1,)), ((), ())), preferred_element_type=F32)
            s = s * scale + madd
            if bias_idx is not None:
                s = s + bias_ref[bias_idx, g]
            m_old = m_ref[g]
            m_new = jnp.maximum(m_old, jnp.max(s, axis=0, keepdims=True))
            m_use = jnp.where(m_new == NEG_INF, 0.0, m_new)
            alpha = jnp.exp(m_old - m_use)
            p = jnp.exp(s - m_use)
            l_ref[g] = alpha * l_ref[g] + jnp.sum(p, axis=0, keepdims=True)
            pv = jnp.dot(vtb, p.astype(BF16), preferred_element_type=F32)
            acc_ref[g] = alpha * acc_ref[g] + pv
            m_ref[g] = m_new

    n_far = jnp.maximum(i - 1, 0)
    per = FAR_KEYS // Q_TILE
    n_big = n_far // per

    def big_body(c, carry):
        chunk(pl.multiple_of(c * FAR_KEYS, FAR_KEYS), FAR_KEYS, None)
        return carry

    lax.fori_loop(0, n_big, big_body, 0)

    def small_body(c, carry):
        chunk(pl.multiple_of(c * Q_TILE, Q_TILE), Q_TILE, None)
        return carry

    lax.fori_loop(n_big * per, n_far, small_body, 0)

    @pl.when(i > 0)
    def _():
        chunk(pl.multiple_of(q0 - Q_TILE, Q_TILE), Q_TILE, 0)

    chunk(pl.multiple_of(q0, Q_TILE), Q_TILE, 1)

    for g in range(N_KV_HEADS):
        out = acc_ref[g] / l_ref[g]
        for hh in range(GROUP):
            h = g * GROUP + hh
            o_ref[:, h * HEAD_DIM:(h + 1) * HEAD_DIM] = (
                out[:, hh * Q_TILE:(hh + 1) * Q_TILE].T.astype(o_ref.dtype))


def _dsa_attention(zb, zc, rel_bias, topk):
    s = zb.shape[0]
    nq = s // Q_TILE
    assert s % FAR_KEYS == 0
    iq = zb[:, ATTN_DIM + 2 * KV_DIM:].reshape(nq, Q_TILE, IDX_HEADS, IDX_DIM)
    iq = iq.transpose(0, 2, 1, 3).reshape(nq, IDX_HEADS * Q_TILE, IDX_DIM)
    vt = zb[:, ATTN_DIM + KV_DIM:ATTN_DIM + 2 * KV_DIM].T
    ik = zc[:, :IDX_DIM].astype(BF16)
    w = zc[:, IDX_DIM:IDX_DIM + IDX_HEADS] * (IDX_HEADS ** -0.5) * (IDX_DIM ** -0.5)
    w = w.reshape(nq, Q_TILE, IDX_HEADS).transpose(0, 2, 1).reshape(nq, 1, IDX_HEADS * Q_TILE)
    bias = _near_bias(rel_bias)
    kcol = ATTN_DIM // KV_DIM

    resident = dict(pipeline_mode=pl.Buffered(1))
    nbytes = (2 * s * KV_DIM * 2 + s * LANES * 2 + bias.size * 4 + s * Q_TILE * 4
              + 2 * (Q_TILE * ATTN_DIM * 2 * 2 + IDX_HEADS * Q_TILE * LANES * 2 + IDX_HEADS * Q_TILE * 4 * SUBLANES)
              + 3 * IDX_KEYS * IDX_HEADS * Q_TILE * 4 + 3 * N_KV_HEADS * HEAD_DIM * GROUP * Q_TILE * 4)
    return pl.pallas_call(
        functools.partial(_dsa_kernel, topk=topk, max_iters=400),
        grid=(nq,),
        in_specs=[pl.BlockSpec((Q_TILE, ATTN_DIM), lambda i: (i, 0)),
                  pl.BlockSpec((s, KV_DIM), lambda i: (0, kcol), **resident),
                  pl.BlockSpec((KV_DIM, s), lambda i: (0, 0), **resident),
                  pl.BlockSpec((None, IDX_HEADS * Q_TILE, IDX_DIM), lambda i: (i, 0, 0)),
                  pl.BlockSpec((s, IDX_DIM), lambda i: (0, 0), **resident),
                  pl.BlockSpec((None, 1, IDX_HEADS * Q_TILE), lambda i: (i, 0, 0)),
                  pl.BlockSpec(bias.shape, lambda i: (0, 0, 0, 0), **resident)],
        out_specs=pl.BlockSpec((Q_TILE, ATTN_DIM), lambda i: (i, 0)),
        out_shape=jax.ShapeDtypeStruct((s, ATTN_DIM), BF16),
        scratch_shapes=[pltpu.VMEM((s, Q_TILE), F32),
                        pltpu.VMEM((N_KV_HEADS, HEAD_DIM, GROUP * Q_TILE), F32),
                        pltpu.VMEM((N_KV_HEADS, 1, GROUP * Q_TILE), F32),
                        pltpu.VMEM((N_KV_HEADS, 1, GROUP * Q_TILE), F32)],
        compiler_params=pltpu.CompilerParams(
            dimension_semantics=("arbitrary",),
            vmem_limit_bytes=_vmem_limit(nbytes)),
        name="dsa_attention",
    )(zb, zb, vt, iq, ik, w, bias)


def _merge_kernel(a_ref, o_ref, wa_ref, wo_ref, ga_ref, gb_ref, out_ref):
    ya = jnp.dot(a_ref[...], wa_ref[...], preferred_element_type=F32)
    yb = jnp.dot(o_ref[...], wo_ref[...], preferred_element_type=F32)
    merged = jax.nn.sigmoid(ga_ref[...]) * ya + jax.nn.sigmoid(gb_ref[...]) * yb
    out_ref[...] = merged.astype(out_ref.dtype)


def _merge(a_in, o, w_conv_out, w_attn_out, zg, tm=512, tn=1024):
    s = a_in.shape[0]
    tm = min(tm, s)
    nn = D_MODEL // tn
    nbytes = 2 * (2 * tm * CONV_DIM * 2 + 2 * CONV_DIM * tn * 2 + 2 * tm * tn * 4 + tm * tn * 2) + 3 * tm * tn * 4
    return pl.pallas_call(
        _merge_kernel,
        grid=(s // tm, nn),
        in_specs=[pl.BlockSpec((tm, CONV_DIM), lambda i, j: (i, 0)),
                  pl.BlockSpec((tm, ATTN_DIM), lambda i, j: (i, 0)),
                  pl.BlockSpec((CONV_DIM, tn), lambda i, j: (0, j)),
                  pl.BlockSpec((ATTN_DIM, tn), lambda i, j: (0, j)),
                  pl.BlockSpec((tm, tn), lambda i, j: (i, j)),
                  pl.BlockSpec((tm, tn), lambda i, j: (i, j + nn))],
        out_specs=pl.BlockSpec((tm, tn), lambda i, j: (i, j)),
        out_shape=jax.ShapeDtypeStruct((s, D_MODEL), BF16),
        compiler_params=pltpu.CompilerParams(
            dimension_semantics=("parallel", "parallel"),
            vmem_limit_bytes=_vmem_limit(nbytes)),
        name="merge",
    )(a_in, o, w_conv_out, w_attn_out, zg, zg)


def _xattn_kernel(h_ref, x_ref, wq_ref, kx_ref, vx_ref, wo_ref, out_ref):
    qx = jnp.dot(h_ref[...], wq_ref[...], preferred_element_type=F32).astype(BF16)
    scale = X_HEAD_DIM ** -0.5
    outs = []
    for h in range(X_HEADS):
        sl = slice(h * X_HEAD_DIM, (h + 1) * X_HEAD_DIM)
        s = lax.dot_general(qx[:, sl], kx_ref[:, sl], (((1,), (1,)), ((), ())),
                            preferred_element_type=F32) * scale
        m = jnp.max(s, axis=-1, keepdims=True)
        p = jnp.exp(s - m)
        l = jnp.sum(p, axis=-1, keepdims=True)
        oh = jnp.dot(p.astype(BF16), vx_ref[:, sl], preferred_element_type=F32) / l
        outs.append(oh.astype(BF16))
    o = jnp.concatenate(outs, axis=1)
    out_ref[...] = x_ref[...] + jnp.dot(o, wo_ref[...], preferred_element_type=F32)


def _cross_attention(hx, x, w_xq, kx, vx, w_xo, tm=256):
    s = hx.shape[0]
    tm = min(tm, s)
    n_mem = kx.shape[0]
    resident = dict(pipeline_mode=pl.Buffered(1))
    nbytes = (2 * (tm * D_MODEL * 2 + 2 * tm * D_MODEL * 4) + 2 * D_MODEL * X_DIM * 2 + 2 * n_mem * X_DIM * 2
              + 2 * tm * D_MODEL * 4)
    return pl.pallas_call(
        _xattn_kernel,
        grid=(s // tm,),
        in_specs=[pl.BlockSpec((tm, D_MODEL), lambda i: (i, 0)),
                  pl.BlockSpec((tm, D_MODEL), lambda i: (i, 0)),
                  pl.BlockSpec((D_MODEL, X_DIM), lambda i: (0, 0), **resident),
                  pl.BlockSpec((n_mem, X_DIM), lambda i: (0, 0), **resident),
                  pl.BlockSpec((n_mem, X_DIM), lambda i: (0, 1), **resident),
                  pl.BlockSpec((X_DIM, D_MODEL), lambda i: (0, 0), **resident)],
        out_specs=pl.BlockSpec((tm, D_MODEL), lambda i: (i, 0)),
        out_shape=jax.ShapeDtypeStruct((s, D_MODEL), F32),
        compiler_params=pltpu.CompilerParams(
            dimension_semantics=("parallel",),
            vmem_limit_bytes=_vmem_limit(nbytes)),
        name="cross_attention",
    )(hx, x, w_xq, kx, kx, w_xo)


def _in_proj_weights(w):
    a_end = 3 * CONV_DIM
    b_end = a_end + ATTN_DIM + 2 * KV_DIM + IDX_HEADS * IDX_DIM
    c_end = b_end + IDX_DIM + IDX_HEADS
    wa = w[:, :a_end].astype(BF16)
    wb = w[:, a_end:b_end].astype(BF16)
    wc = jnp.pad(w[:, b_end:c_end], ((0, 0), (0, LANES - (c_end - b_end)))).astype(BF16)
    wg = w[:, c_end:].astype(BF16)
    return wa, wb, wc, wg


def kernel(x, mem, rel_bias, norm_mix, w_in, conv_w, w_conv_out, w_attn_out, w_mix_out, norm_xattn, norm_mem, w_xq, w_xkv, w_xo, norm_mlp, w_up, w_down, norm_final):
    bsz, s, d = x.shape
    assert bsz == 1 and d == D_MODEL
    depth = w_in.shape[0]
    topk = min(TOPK_MAX, s // 4)
    xs = x.reshape(s, d)
    mems = mem.reshape(mem.shape[1], d)
    for l in range(depth):
        wa, wb, wc, wg = _in_proj_weights(w_in[l])
        h = _rmsnorm(xs, norm_mix[l], BF16)
        za = _matmul(h, wa, out_dtype=F32, tm=1024, tn=1024, name="proj_conv")
        zb = _matmul(h, wb, out_dtype=BF16, tm=1024, tn=1024, name="proj_attn")
        zc = _matmul(h, wc, out_dtype=F32, tm=1024, tn=LANES, name="proj_idx")
        zg = _matmul(h, wg, out_dtype=F32, tm=1024, tn=1024, name="proj_gate")
        a_in = _gated_conv(za, conv_w[l])
        o = _dsa_attention(zb, zc, rel_bias, topk)
        merged = _merge(a_in, o, w_conv_out[l].astype(BF16), w_attn_out[l].astype(BF16), zg)
        xs = _matmul(merged, w_mix_out[l].astype(BF16), out_dtype=F32, tm=1024, tn=1024,
                     epilogue="residual", residual=xs, name="mix_out")
        hx = _rmsnorm(xs, norm_xattn[l], BF16)
        hm = _rmsnorm(mems, norm_mem[l], BF16)
        kv = _matmul(hm, w_xkv[l].astype(BF16), out_dtype=BF16, tm=256, tn=1024, name="mem_kv")
        xs = _cross_attention(hx, xs, w_xq[l].astype(BF16), kv, kv, w_xo[l].astype(BF16))
        hmlp = _rmsnorm(xs, norm_mlp[l], BF16)
        act = _matmul(hmlp, w_up[l].astype(BF16), out_dtype=BF16, tm=1024, tn=1024,
                      epilogue="relu2", name="mlp_up")
        xs = _matmul(act, w_down[l].astype(BF16), out_dtype=F32, tm=1024, tn=1024, tk=2048,
                     epilogue="residual", residual=xs, name="mlp_down")
    out = _rmsnorm(xs, norm_final, F32)
    return out.reshape(bsz, s, d)
```

```python
import functools
import math

import numpy as np
import jax
import jax.numpy as jnp
from jax import lax
from jax.experimental import pallas as pl
from jax.experimental.pallas import tpu as pltpu

F32 = jnp.float32
BF16 = jnp.bfloat16

D_MODEL = 4096
CONV_DIM = 2048
CONV_K = 3
N_HEADS = 16
N_KV_HEADS = 4
HEAD_DIM = 128
GROUP = N_HEADS // N_KV_HEADS
ATTN_DIM = N_HEADS * HEAD_DIM
KV_DIM = N_KV_HEADS * HEAD_DIM
IDX_HEADS = 32
IDX_DIM = 64
IDX_ALL = IDX_HEADS * IDX_DIM
TOPK_MAX = 256
N_BUCKETS = 32
MAX_DISTANCE = 128
X_HEADS = 4
X_HEAD_DIM = 128
X_DIM = X_HEADS * X_HEAD_DIM
EPS = 1e-6

V7X_VMEM_BYTES = 64 * 1024 * 1024
LANES = 128
SUBLANES = 8

Q_TILE = 128
IDX_KEYS = 256
ATT_KEYS = 512
V_ROWS = HEAD_DIM + 16
NEG_INF = float("-inf")
NT_DIMS = (((1,), (1,)), ((), ()))


def _vmem_limit(nbytes):
    return int(min(nbytes + (8 << 20), V7X_VMEM_BYTES - (6 << 20)))


def _rmsnorm_kernel(x_ref, g_ref, o_ref):
    x = x_ref[...]
    inv = lax.rsqrt(jnp.mean(x * x, axis=-1, keepdims=True) + EPS)
    o_ref[...] = ((x * inv) * g_ref[...]).astype(o_ref.dtype)


def _rmsnorm(x, g, out_dtype, tr=256):
    rows, d = x.shape
    tr = min(tr, rows)
    return pl.pallas_call(
        _rmsnorm_kernel,
        grid=(rows // tr,),
        in_specs=[pl.BlockSpec((tr, d), lambda i: (i, 0)),
                  pl.BlockSpec((1, d), lambda i: (0, 0))],
        out_specs=pl.BlockSpec((tr, d), lambda i: (i, 0)),
        out_shape=jax.ShapeDtypeStruct((rows, d), out_dtype),
        compiler_params=pltpu.CompilerParams(
            dimension_semantics=("parallel",),
            vmem_limit_bytes=_vmem_limit(2 * tr * d * (4 + jnp.dtype(out_dtype).itemsize))),
        name="rmsnorm",
    )(x, g.reshape(1, d))


def _mm_kernel(*refs, nk, epilogue):
    if epilogue == "residual":
        a_ref, b_ref, r_ref, o_ref = refs[:4]
        scratch = refs[4:]
    else:
        a_ref, b_ref, o_ref = refs[:3]
        r_ref = None
        scratch = refs[3:]

    def finish(acc):
        if epilogue == "relu2":
            acc = jnp.square(jnp.maximum(acc, 0.0))
        elif epilogue == "residual":
            acc = r_ref[...] + acc
        o_ref[...] = acc.astype(o_ref.dtype)

    d = jnp.dot(a_ref[...], b_ref[...], preferred_element_type=F32)
    if nk == 1:
        finish(d)
        return
    acc_ref, = scratch
    k = pl.program_id(2)

    @pl.when(k == 0)
    def _():
        acc_ref[...] = d

    @pl.when(jnp.logical_and(k > 0, k < nk - 1))
    def _():
        acc_ref[...] += d

    @pl.when(k == nk - 1)
    def _():
        finish(acc_ref[...] + d)


def _matmul(a, b, *, out_dtype, tm, tn, tk=None, epilogue=None, residual=None, name="matmul"):
    m, kdim = a.shape
    n = b.shape[1]
    tm, tn = min(tm, m), min(tn, n)
    tk = kdim if tk is None else tk
    nk = kdim // tk
    assert m % tm == 0 and n % tn == 0 and kdim % tk == 0
    in_specs = [pl.BlockSpec((tm, tk), lambda i, j, k: (i, k)),
                pl.BlockSpec((tk, tn), lambda i, j, k: (k, j))]
    args = [a, b]
    nbytes = 2 * (tm * tk + tk * tn) * 2 + 2 * tm * tn * jnp.dtype(out_dtype).itemsize
    if epilogue == "residual":
        in_specs.append(pl.BlockSpec((tm, tn), lambda i, j, k: (i, j)))
        args.append(residual)
        nbytes += 2 * tm * tn * 4
    scratch = []
    if nk > 1:
        scratch.append(pltpu.VMEM((tm, tn), F32))
        nbytes += tm * tn * 4
    nbytes += tm * tn * 4
    return pl.pallas_call(
        functools.partial(_mm_kernel, nk=nk, epilogue=epilogue),
        grid=(m // tm, n // tn, nk),
        in_specs=in_specs,
        out_specs=pl.BlockSpec((tm, tn), lambda i, j, k: (i, j)),
        out_shape=jax.ShapeDtypeStruct((m, n), out_dtype),
        scratch_shapes=scratch,
        compiler_params=pltpu.CompilerParams(
            dimension_semantics=("parallel", "parallel", "arbitrary"),
            vmem_limit_bytes=_vmem_limit(nbytes)),
        name=name,
    )(*args)


def _conv_kernel(u_ref, cb_ref, cc_ref, up_ref, ccp_ref, w_ref, o_ref, p_ref, *, tr):
    i = pl.program_id(0)
    p = cc_ref[...] * u_ref[...]
    prev = jnp.where(i > 0, ccp_ref[...] * up_ref[...], 0.0)
    p_ref[0:SUBLANES, :] = prev
    p_ref[SUBLANES:SUBLANES + tr, :] = p
    p1 = p_ref[SUBLANES - 1:SUBLANES - 1 + tr, :]
    p2 = p_ref[SUBLANES - 2:SUBLANES - 2 + tr, :]
    w = w_ref[...]
    z = w[2:3, :] * p + w[0:1, :] * p2 + w[1:2, :] * p1
    o_ref[...] = (cb_ref[...] * z).astype(o_ref.dtype)


def _gated_conv(za, conv_w, tr=512, tc=512):
    s = za.shape[0]
    tr = min(tr, s)
    nc = CONV_DIM // tc
    rb = tr // SUBLANES

    def prev_map(off):
        return lambda i, c: (jnp.maximum(i * rb - 1, 0), c + off)

    return pl.pallas_call(
        functools.partial(_conv_kernel, tr=tr),
        grid=(s // tr, nc),
        in_specs=[pl.BlockSpec((tr, tc), lambda i, c: (i, c)),
                  pl.BlockSpec((tr, tc), lambda i, c: (i, c + nc)),
                  pl.BlockSpec((tr, tc), lambda i, c: (i, c + 2 * nc)),
                  pl.BlockSpec((SUBLANES, tc), prev_map(0)),
                  pl.BlockSpec((SUBLANES, tc), prev_map(2 * nc)),
                  pl.BlockSpec((CONV_K, tc), lambda i, c: (0, c))],
        out_specs=pl.BlockSpec((tr, tc), lambda i, c: (i, c)),
        out_shape=jax.ShapeDtypeStruct((s, CONV_DIM), BF16),
        scratch_shapes=[pltpu.VMEM((tr + SUBLANES, tc), F32)],
        compiler_params=pltpu.CompilerParams(
            dimension_semantics=("parallel", "parallel"),
            vmem_limit_bytes=_vmem_limit(10 * tr * tc * 4)),
        name="gated_conv",
    )(za, za, za, za, za, conv_w)


def _t5_bucket_np(dist):
    n = np.maximum(dist, 0)
    max_exact = N_BUCKETS // 2
    nf = np.maximum(n, 1).astype(np.float64)
    large = max_exact + (np.log(nf / max_exact) / math.log(MAX_DISTANCE / max_exact)
                         * (N_BUCKETS - max_exact)).astype(np.int32)
    large = np.minimum(large, N_BUCKETS - 1)
    return np.where(n < max_exact, n, large).astype(np.int32)


def _near_bias(rel_bias):
    sl = np.arange(Q_TILE)[:, None]
    tl = np.arange(Q_TILE)[None, :]
    dist = np.stack([tl - sl + Q_TILE, tl - sl])
    assert _t5_bucket_np(np.array([Q_TILE]))[0] == N_BUCKETS - 1
    bucket = _t5_bucket_np(dist)
    rel = (rel_bias - rel_bias[N_BUCKETS - 1][None, :]) * (HEAD_DIM ** 0.5)
    b = rel[bucket]
    b = b.reshape(2, Q_TILE, Q_TILE, N_KV_HEADS, GROUP)
    b = b.transpose(0, 3, 1, 4, 2)
    b = b.reshape(2, N_KV_HEADS, Q_TILE, GROUP * Q_TILE).astype(F32)
    return jnp.concatenate([jnp.zeros_like(b[:1]), b], axis=0)


def _dsa_kernel(iq_ref, q_ref, k_ref, vt_ref, ik2_ref, w_ref, bias_ref, o_ref,
                sc_ref, s_ref, acc_ref, m_ref, *, topk, max_iters):
    i = pl.program_id(0)
    q0 = i * Q_TILE
    kf = float(topk)
    per = ATT_KEYS // Q_TILE
    c_last = i // per
    n_cnt = c_last + 1
    n_idx = (q0 + Q_TILE + IDX_KEYS - 1) // IDX_KEYS
    qpos = q0 + lax.broadcasted_iota(jnp.int32, (1, Q_TILE), 1)

    def fold8(x, op):
        rows = x.shape[0]
        if rows > SUBLANES * SUBLANES:
            x = op(x.reshape(SUBLANES, rows // SUBLANES, x.shape[1]), axis=0)
        return op(x.reshape(x.shape[0] // SUBLANES, SUBLANES, x.shape[1]), axis=0)

    def col_reduce(x, op):
        return op(fold8(x, op), axis=0, keepdims=True)

    n_dots = IDX_ALL // (2 * LANES)
    rhs = [jnp.concatenate([iq_ref[:, (2 * d) * LANES:(2 * d + 1) * LANES],
                            iq_ref[:, (2 * d + 1) * LANES:(2 * d + 2) * LANES]], axis=0)
           for d in range(n_dots)]

    def idx_body(j, carry):
        mx, mn = carry
        s0 = pl.multiple_of(j * IDX_KEYS, IDX_KEYS)
        lhs = jnp.concatenate([ik2_ref[pl.ds(s0, IDX_KEYS), 0:LANES],
                               ik2_ref[pl.ds(s0, IDX_KEYS), LANES:2 * LANES]], axis=0)
        acc = None
        for d in range(n_dots):
            r = lax.dot_general(lhs, rhs[d], NT_DIMS, preferred_element_type=F32)
            r = jnp.maximum(r, 0.0)
            t = r[:IDX_KEYS] * w_ref[2 * d:2 * d + 1, :] + r[IDX_KEYS:] * w_ref[2 * d + 1:2 * d + 2, :]
            acc = t if acc is None else acc + t
        acc = acc[:, :Q_TILE] + acc[:, Q_TILE:]
        kpos = s0 + lax.broadcasted_iota(jnp.int32, (IDX_KEYS, 1), 0)
        causal = kpos <= qpos
        lo_fill = jnp.where(causal, acc, NEG_INF)
        sc_ref[pl.ds(s0, IDX_KEYS), :] = lo_fill
        mx = jnp.maximum(mx, fold8(lo_fill, jnp.max))
        mn = jnp.minimum(mn, fold8(jnp.where(causal, acc, -NEG_INF), jnp.min))
        return mx, mn

    mx8, mn8 = lax.fori_loop(
        0, n_idx, idx_body,
        (jnp.full((SUBLANES, Q_TILE), NEG_INF, F32), jnp.full((SUBLANES, Q_TILE), -NEG_INF, F32)))
    hi0 = jnp.max(mx8, axis=0, keepdims=True)
    lo0 = jnp.min(mn8, axis=0, keepdims=True)

    def fill_body(j, carry):
        sc_ref[pl.ds(pl.multiple_of(j * IDX_KEYS, IDX_KEYS), IDX_KEYS), :] = jnp.full(
            (IDX_KEYS, Q_TILE), NEG_INF, F32)
        return carry

    lax.fori_loop(n_idx, n_cnt * (ATT_KEYS // IDX_KEYS), fill_body, 0)

    def count_ge(tau):
        def body(c, cnt):
            blk = sc_ref[pl.ds(pl.multiple_of(c * ATT_KEYS, ATT_KEYS), ATT_KEYS), :]
            return cnt + fold8(jnp.where(blk >= tau, 1.0, 0.0), jnp.sum)
        c8 = lax.fori_loop(0, n_cnt, body, jnp.zeros((SUBLANES, Q_TILE), F32))
        return jnp.sum(c8, axis=0, keepdims=True)

    def n_active(lo, hi, flo):
        mid = 0.5 * lo + 0.5 * hi
        act = jnp.logical_and(flo > kf, jnp.logical_and(mid > lo, mid < hi))
        return act, jnp.sum(jnp.where(act, 1.0, 0.0))

    flo0 = (qpos + 1).astype(F32)

    def bis_cond(st):
        _, _, _, n, it = st
        return jnp.logical_and(n > 0.0, it < max_iters)

    def bis_body(st):
        lo, hi, flo, _, it = st
        act, _ = n_active(lo, hi, flo)
        mid = 0.5 * lo + 0.5 * hi
        c = count_ge(mid)
        up = jnp.logical_and(act, c >= kf)
        dn = jnp.logical_and(act, c < kf)
        lo = jnp.where(up, mid, lo)
        flo = jnp.where(up, c, flo)
        hi = jnp.where(dn, mid, hi)
        _, n = n_active(lo, hi, flo)
        return lo, hi, flo, n, it + 1

    _, n0 = n_active(lo0, hi0, flo0)
    lo, hi, flo, _, _ = lax.while_loop(bis_cond, bis_body, (lo0, hi0, flo0, n0, jnp.int32(0)))
    open_hi = jnp.logical_and(flo > kf, hi == hi0)
    n_open = jnp.sum(jnp.where(open_hi, 1.0, 0.0))
    c_hi = lax.cond(n_open > 0.0, lambda: count_ge(hi), lambda: jnp.zeros((1, Q_TILE), F32))
    tau = jnp.where(jnp.logical_and(open_hi, c_hi >= kf), hi, lo)

    c2 = (HEAD_DIM ** -0.5) * math.log2(math.e)
    qg = []
    for g in range(N_KV_HEADS):
        qg.append(jnp.concatenate(
            [q_ref[:, (g * GROUP + hh) * HEAD_DIM:(g * GROUP + hh + 1) * HEAD_DIM] for hh in range(GROUP)],
            axis=0))

    m_ref[...] = jnp.full(m_ref.shape, NEG_INF, F32)
    acc_ref[...] = jnp.zeros(acc_ref.shape, F32)

    def logits(c, g, slot):
        s0 = pl.multiple_of(jnp.minimum(c, n_cnt - 1) * ATT_KEYS, ATT_KEYS)
        kb = k_ref[pl.ds(s0, ATT_KEYS), g * HEAD_DIM:(g + 1) * HEAD_DIM]
        s_ref[slot] = lax.dot_general(kb, qg[g], NT_DIMS, preferred_element_type=F32)

    def chunk(c, near):
        s0 = pl.multiple_of(c * ATT_KEYS, ATT_KEYS)
        blk = sc_ref[pl.ds(s0, ATT_KEYS), :]
        madd = jnp.where(blk >= tau, 0.0, NEG_INF)
        madd = jnp.concatenate([madd] * GROUP, axis=1)
        for g in range(N_KV_HEADS):
            if g + 1 < N_KV_HEADS:
                logits(c, g + 1, (g + 1) % 2)
            else:
                logits(c + 1, 0, 0)
            vtb = vt_ref[g, :, pl.ds(s0, ATT_KEYS)]
            s = s_ref[g % 2] + madd
            if near:
                rows = []
                for r in range(per):
                    d = i - (c * per + r)
                    rows.append(bias_ref[jnp.where(d == 1, 1, jnp.where(d == 0, 2, 0)), g])
                s = s + jnp.concatenate(rows, axis=0)
            m_old = m_ref[g]
            m_new = jnp.maximum(m_old, col_reduce(s, jnp.max))
            m_use = jnp.where(m_new == NEG_INF, 0.0, m_new)
            alpha = jnp.exp2(c2 * (m_old - m_use))
            p = jnp.exp2(c2 * (s - m_use))
            pv = jnp.dot(vtb, p.astype(BF16), preferred_element_type=F32)
            acc_ref[g] = alpha * acc_ref[g] + pv
            m_ref[g] = m_new

    n_far = jnp.maximum(c_last - 1, 0)
    logits(0, 0, 0)

    def far_body(c, carry):
        chunk(c, False)
        return carry

    lax.fori_loop(0, n_far, far_body, 0)

    def near_body(c, carry):
        chunk(c, True)
        return carry

    lax.fori_loop(n_far, n_cnt, near_body, 0)

    for g in range(N_KV_HEADS):
        out = acc_ref[g, 0:HEAD_DIM, :] / acc_ref[g, HEAD_DIM:HEAD_DIM + 1, :]
        for hh in range(GROUP):
            h = g * GROUP + hh
            o_ref[:, h * HEAD_DIM:(h + 1) * HEAD_DIM] = (
                out[:, hh * Q_TILE:(hh + 1) * Q_TILE].T.astype(o_ref.dtype))


def _dsa_attention(zb, zc, rel_bias, topk):
    s = zb.shape[0]
    nq = s // Q_TILE
    assert s % ATT_KEYS == 0
    k_col = (IDX_ALL + ATTN_DIM) // KV_DIM
    vt = zb[:, IDX_ALL + ATTN_DIM + KV_DIM:].T.reshape(N_KV_HEADS, HEAD_DIM, s)
    vt = jnp.concatenate([vt, jnp.ones((N_KV_HEADS, V_ROWS - HEAD_DIM, s), BF16)], axis=1)
    ik2 = zc[:, :2 * LANES].astype(BF16)
    w = zc[:, 2 * LANES:2 * LANES + IDX_HEADS] * (IDX_HEADS ** -0.5) * (IDX_DIM ** -0.5)
    w = w.reshape(nq, Q_TILE, IDX_HEADS // 4, 2, 2).transpose(0, 2, 4, 3, 1)
    w = w.reshape(nq, IDX_HEADS // 2, 2 * Q_TILE)
    bias = _near_bias(rel_bias)

    resident = dict(pipeline_mode=pl.Buffered(1))
    nbytes = (s * KV_DIM * 2 + vt.size * 2 + s * 2 * LANES * 2 + bias.size * 4 + s * Q_TILE * 4
              + 2 * (2 * Q_TILE * ATTN_DIM * 2 * 2 + IDX_HEADS * Q_TILE * 4)
              + 6 * ATT_KEYS * GROUP * Q_TILE * 4 + 3 * N_KV_HEADS * V_ROWS * GROUP * Q_TILE * 4)
    return pl.pallas_call(
        functools.partial(_dsa_kernel, topk=topk, max_iters=400),
        grid=(nq,),
        in_specs=[pl.BlockSpec((Q_TILE, IDX_ALL), lambda i: (i, 0)),
                  pl.BlockSpec((Q_TILE, ATTN_DIM), lambda i: (i, 1)),
                  pl.BlockSpec((s, KV_DIM), lambda i: (0, k_col), **resident),
                  pl.BlockSpec(vt.shape, lambda i: (0, 0, 0), **resident),
                  pl.BlockSpec((s, 2 * LANES), lambda i: (0, 0), **resident),
                  pl.BlockSpec((None, IDX_HEADS // 2, 2 * Q_TILE), lambda i: (i, 0, 0)),
                  pl.BlockSpec(bias.shape, lambda i: (0, 0, 0, 0), **resident)],
        out_specs=pl.BlockSpec((Q_TILE, ATTN_DIM), lambda i: (i, 0)),
        out_shape=jax.ShapeDtypeStruct((s, ATTN_DIM), BF16),
        scratch_shapes=[pltpu.VMEM((s, Q_TILE), F32),
                        pltpu.VMEM((2, ATT_KEYS, GROUP * Q_TILE), F32),
                        pltpu.VMEM((N_KV_HEADS, V_ROWS, GROUP * Q_TILE), F32),
                        pltpu.VMEM((N_KV_HEADS, 1, GROUP * Q_TILE), F32)],
        compiler_params=pltpu.CompilerParams(
            dimension_semantics=("arbitrary",),
            vmem_limit_bytes=_vmem_limit(nbytes)),
        name="dsa_attention",
    )(zb, zb, zb, vt, ik2, w, bias)


def _merge_kernel(a_ref, o_ref, wa_ref, wo_ref, ga_ref, gb_ref, out_ref):
    ya = jnp.dot(a_ref[...], wa_ref[...], preferred_element_type=F32)
    yb = jnp.dot(o_ref[...], wo_ref[...], preferred_element_type=F32)
    merged = jax.nn.sigmoid(ga_ref[...]) * ya + jax.nn.sigmoid(gb_ref[...]) * yb
    out_ref[...] = merged.astype(out_ref.dtype)


def _merge(a_in, o, w_conv_out, w_attn_out, zg, tm=512, tn=1024):
    s = a_in.shape[0]
    tm = min(tm, s)
    nn = D_MODEL // tn
    nbytes = 2 * (2 * tm * CONV_DIM * 2 + 2 * CONV_DIM * tn * 2 + 2 * tm * tn * 4 + tm * tn * 2) + 3 * tm * tn * 4
    return pl.pallas_call(
        _merge_kernel,
        grid=(s // tm, nn),
        in_specs=[pl.BlockSpec((tm, CONV_DIM), lambda i, j: (i, 0)),
                  pl.BlockSpec((tm, ATTN_DIM), lambda i, j: (i, 0)),
                  pl.BlockSpec((CONV_DIM, tn), lambda i, j: (0, j)),
                  pl.BlockSpec((ATTN_DIM, tn), lambda i, j: (0, j)),
                  pl.BlockSpec((tm, tn), lambda i, j: (i, j)),
                  pl.BlockSpec((tm, tn), lambda i, j: (i, j + nn))],
        out_specs=pl.BlockSpec((tm, tn), lambda i, j: (i, j)),
        out_shape=jax.ShapeDtypeStruct((s, D_MODEL), BF16),
        compiler_params=pltpu.CompilerParams(
            dimension_semantics=("parallel", "parallel"),
            vmem_limit_bytes=_vmem_limit(nbytes)),
        name="merge",
    )(a_in, o, w_conv_out, w_attn_out, zg, zg)


def _xattn_kernel(h_ref, x_ref, wq_ref, kx_ref, vx_ref, wo_ref, out_ref):
    qx = jnp.dot(h_ref[...], wq_ref[...], preferred_element_type=F32).astype(BF16)
    scale = X_HEAD_DIM ** -0.5
    outs = []
    for h in range(X_HEADS):
        sl = slice(h * X_HEAD_DIM, (h + 1) * X_HEAD_DIM)
        s = lax.dot_general(qx[:, sl], kx_ref[:, sl], NT_DIMS, preferred_element_type=F32) * scale
        m = jnp.max(s, axis=-1, keepdims=True)
        p = jnp.exp(s - m)
        l = jnp.sum(p, axis=-1, keepdims=True)
        oh = jnp.dot(p.astype(BF16), vx_ref[:, sl], preferred_element_type=F32) / l
        outs.append(oh.astype(BF16))
    o = jnp.concatenate(outs, axis=1)
    out_ref[...] = x_ref[...] + jnp.dot(o, wo_ref[...], preferred_element_type=F32)


def _cross_attention(hx, x, w_xq, kv, w_xo, tm=256):
    s = hx.shape[0]
    tm = min(tm, s)
    n_mem = kv.shape[0]
    resident = dict(pipeline_mode=pl.Buffered(1))
    nbytes = (2 * (tm * D_MODEL * 2 + 2 * tm * D_MODEL * 4) + 2 * D_MODEL * X_DIM * 2 + 2 * n_mem * X_DIM * 2
              + 2 * tm * D_MODEL * 4)
    return pl.pallas_call(
        _xattn_kernel,
        grid=(s // tm,),
        in_specs=[pl.BlockSpec((tm, D_MODEL), lambda i: (i, 0)),
                  pl.BlockSpec((tm, D_MODEL), lambda i: (i, 0)),
                  pl.BlockSpec((D_MODEL, X_DIM), lambda i: (0, 0), **resident),
                  pl.BlockSpec((n_mem, X_DIM), lambda i: (0, 0), **resident),
                  pl.BlockSpec((n_mem, X_DIM), lambda i: (0, 1), **resident),
                  pl.BlockSpec((X_DIM, D_MODEL), lambda i: (0, 0), **resident)],
        out_specs=pl.BlockSpec((tm, D_MODEL), lambda i: (i, 0)),
        out_shape=jax.ShapeDtypeStruct((s, D_MODEL), F32),
        compiler_params=pltpu.CompilerParams(
            dimension_semantics=("parallel",),
            vmem_limit_bytes=_vmem_limit(nbytes)),
        name="cross_attention",
    )(hx, x, w_xq, kv, kv, w_xo)


def _in_proj_weights(w):
    a_end = 3 * CONV_DIM
    q_end = a_end + ATTN_DIM
    kv_end = q_end + 2 * KV_DIM
    iq_end = kv_end + IDX_ALL
    ik_end = iq_end + IDX_DIM
    iw_end = ik_end + IDX_HEADS
    wa = w[:, :a_end].astype(BF16)
    wb = jnp.concatenate([w[:, kv_end:iq_end], w[:, a_end:kv_end]], axis=1).astype(BF16)
    w_ik = w[:, iq_end:ik_end]
    zeros = jnp.zeros((w.shape[0], 2 * IDX_DIM), w.dtype)
    pad = jnp.zeros((w.shape[0], LANES - IDX_HEADS), w.dtype)
    wc = jnp.concatenate([w_ik, zeros, w_ik, w[:, ik_end:iw_end], pad], axis=1).astype(BF16)
    wg = w[:, iw_end:].astype(BF16)
    return wa, wb, wc, wg


def kernel(x, mem, rel_bias, norm_mix, w_in, conv_w, w_conv_out, w_attn_out, w_mix_out, norm_xattn, norm_mem, w_xq, w_xkv, w_xo, norm_mlp, w_up, w_down, norm_final):
    bsz, s, d = x.shape
    assert bsz == 1 and d == D_MODEL
    depth = w_in.shape[0]
    topk = min(TOPK_MAX, s // 4)
    xs = x.reshape(s, d)
    mems = mem.reshape(mem.shape[1], d)
    for l in range(depth):
        wa, wb, wc, wg = _in_proj_weights(w_in[l])
        h = _rmsnorm(xs, norm_mix[l], BF16)
        za = _matmul(h, wa, out_dtype=F32, tm=1024, tn=1024, name="proj_conv")
        zb = _matmul(h, wb, out_dtype=BF16, tm=1024, tn=1024, name="proj_attn")
        zc = _matmul(h, wc, out_dtype=F32, tm=1024, tn=3 * LANES, name="proj_idx")
        zg = _matmul(h, wg, out_dtype=F32, tm=1024, tn=1024, name="proj_gate")
        a_in = _gated_conv(za, conv_w[l])
        o = _dsa_attention(zb, zc, rel_bias, topk)
        merged = _merge(a_in, o, w_conv_out[l].astype(BF16), w_attn_out[l].astype(BF16), zg)
        xs = _matmul(merged, w_mix_out[l].astype(BF16), out_dtype=F32, tm=1024, tn=1024,
                     epilogue="residual", residual=xs, name="mix_out")
        hx = _rmsnorm(xs, norm_xattn[l], BF16)
        hm = _rmsnorm(mems, norm_mem[l], BF16)
        kv = _matmul(hm, w_xkv[l].astype(BF16), out_dtype=BF16, tm=256, tn=1024, name="mem_kv")
        xs = _cross_attention(hx, xs, w_xq[l].astype(BF16), kv, w_xo[l].astype(BF16))
        hmlp = _rmsnorm(xs, norm_mlp[l], BF16)
        act = _matmul(hmlp, w_up[l].astype(BF16), out_dtype=BF16, tm=1024, tn=1024,
                      epilogue="relu2", name="mlp_up")
        xs = _matmul(act, w_down[l].astype(BF16), out_dtype=F32, tm=1024, tn=1024, tk=2048,
                     epilogue="residual", residual=xs, name="mlp_down")
    out = _rmsnorm(xs, norm_final, F32)
    return out.reshape(bsz, s, d)
```

```python
import functools
import math

import numpy as np
import jax
import jax.numpy as jnp
from jax import lax
from jax.experimental import pallas as pl
from jax.experimental.pallas import tpu as pltpu

F32 = jnp.float32
BF16 = jnp.bfloat16

D_MODEL = 4096
CONV_DIM = 2048
CONV_K = 3
N_HEADS = 16
N_KV_HEADS = 4
HEAD_DIM = 128
GROUP = N_HEADS // N_KV_HEADS
ATTN_DIM = N_HEADS * HEAD_DIM
KV_DIM = N_KV_HEADS * HEAD_DIM
IDX_HEADS = 32
IDX_DIM = 64
IDX_ALL = IDX_HEADS * IDX_DIM
TOPK_MAX = 256
N_BUCKETS = 32
MAX_DISTANCE = 128
X_HEADS = 4
X_HEAD_DIM = 128
X_DIM = X_HEADS * X_HEAD_DIM
EPS = 1e-6

V7X_VMEM_BYTES = 64 * 1024 * 1024
LANES = 128
SUBLANES = 8

Q_TILE = 128
IDX_KEYS = 256
ATT_KEYS = 512
V_ROWS = HEAD_DIM + 16
NEG_INF = float("-inf")
NT_DIMS = (((1,), (1,)), ((), ()))


def _vmem_limit(nbytes):
    return int(min(nbytes + (8 << 20), V7X_VMEM_BYTES - (6 << 20)))


def _rmsnorm_kernel(x_ref, g_ref, o_ref):
    x = x_ref[...]
    inv = lax.rsqrt(jnp.mean(x * x, axis=-1, keepdims=True) + EPS)
    o_ref[...] = ((x * inv) * g_ref[...]).astype(o_ref.dtype)


def _rmsnorm(x, g, out_dtype, tr=256):
    rows, d = x.shape
    tr = min(tr, rows)
    return pl.pallas_call(
        _rmsnorm_kernel,
        grid=(rows // tr,),
        in_specs=[pl.BlockSpec((tr, d), lambda i: (i, 0)),
                  pl.BlockSpec((1, d), lambda i: (0, 0))],
        out_specs=pl.BlockSpec((tr, d), lambda i: (i, 0)),
        out_shape=jax.ShapeDtypeStruct((rows, d), out_dtype),
        compiler_params=pltpu.CompilerParams(
            dimension_semantics=("parallel",),
            vmem_limit_bytes=_vmem_limit(2 * tr * d * (4 + jnp.dtype(out_dtype).itemsize))),
        name="rmsnorm",
    )(x, g.reshape(1, d))


def _mm_kernel(*refs, nk, epilogue):
    if epilogue == "residual":
        a_ref, b_ref, r_ref, o_ref = refs[:4]
        scratch = refs[4:]
    else:
        a_ref, b_ref, o_ref = refs[:3]
        r_ref = None
        scratch = refs[3:]

    def finish(acc):
        if epilogue == "relu2":
            acc = jnp.square(jnp.maximum(acc, 0.0))
        elif epilogue == "residual":
            acc = r_ref[...] + acc
        o_ref[...] = acc.astype(o_ref.dtype)

    d = jnp.dot(a_ref[...], b_ref[...].astype(BF16), preferred_element_type=F32)
    if nk == 1:
        finish(d)
        return
    acc_ref, = scratch
    k = pl.program_id(2)

    @pl.when(k == 0)
    def _():
        acc_ref[...] = d

    @pl.when(jnp.logical_and(k > 0, k < nk - 1))
    def _():
        acc_ref[...] += d

    @pl.when(k == nk - 1)
    def _():
        finish(acc_ref[...] + d)


def _matmul(a, b, *, out_dtype, tm, tn, tk=None, layer=None, col0=0, n=None,
            epilogue=None, residual=None, name="matmul"):
    m, kdim = a.shape
    n = (b.shape[-1] - col0) if n is None else n
    tm, tn = min(tm, m), min(tn, n)
    tk = kdim if tk is None else tk
    nk = kdim // tk
    assert m % tm == 0 and n % tn == 0 and kdim % tk == 0 and col0 % tn == 0
    j0 = col0 // tn
    if layer is None:
        b_spec = pl.BlockSpec((tk, tn), lambda i, j, k: (k, j + j0))
    else:
        b_spec = pl.BlockSpec((None, tk, tn), lambda i, j, k: (layer, k, j + j0))
    in_specs = [pl.BlockSpec((tm, tk), lambda i, j, k: (i, k)), b_spec]
    args = [a, b]
    b_item = jnp.dtype(b.dtype).itemsize
    nbytes = 2 * tm * tk * 2 + 2 * tk * tn * b_item + 2 * tm * tn * jnp.dtype(out_dtype).itemsize
    if b_item != 2:
        nbytes += tk * tn * 2
    if epilogue == "residual":
        in_specs.append(pl.BlockSpec((tm, tn), lambda i, j, k: (i, j)))
        args.append(residual)
        nbytes += 2 * tm * tn * 4
    scratch = []
    if nk > 1:
        scratch.append(pltpu.VMEM((tm, tn), F32))
        nbytes += tm * tn * 4
    nbytes += tm * tn * 4
    return pl.pallas_call(
        functools.partial(_mm_kernel, nk=nk, epilogue=epilogue),
        grid=(m // tm, n // tn, nk),
        in_specs=in_specs,
        out_specs=pl.BlockSpec((tm, tn), lambda i, j, k: (i, j)),
        out_shape=jax.ShapeDtypeStruct((m, n), out_dtype),
        scratch_shapes=scratch,
        compiler_params=pltpu.CompilerParams(
            dimension_semantics=("parallel", "parallel", "arbitrary"),
            vmem_limit_bytes=_vmem_limit(nbytes)),
        name=name,
    )(*args)


def _conv_kernel(u_ref, cb_ref, cc_ref, up_ref, ccp_ref, w_ref, o_ref, p_ref, *, tr):
    i = pl.program_id(0)
    p = cc_ref[...] * u_ref[...]
    prev = jnp.where(i > 0, ccp_ref[...] * up_ref[...], 0.0)
    p_ref[0:SUBLANES, :] = prev
    p_ref[SUBLANES:SUBLANES + tr, :] = p
    p1 = p_ref[SUBLANES - 1:SUBLANES - 1 + tr, :]
    p2 = p_ref[SUBLANES - 2:SUBLANES - 2 + tr, :]
    w = w_ref[...]
    z = w[2:3, :] * p + w[0:1, :] * p2 + w[1:2, :] * p1
    o_ref[...] = (cb_ref[...] * z).astype(o_ref.dtype)


def _gated_conv(za, conv_w, tr=512, tc=512):
    s = za.shape[0]
    tr = min(tr, s)
    nc = CONV_DIM // tc
    rb = tr // SUBLANES

    def prev_map(off):
        return lambda i, c: (jnp.maximum(i * rb - 1, 0), c + off)

    return pl.pallas_call(
        functools.partial(_conv_kernel, tr=tr),
        grid=(s // tr, nc),
        in_specs=[pl.BlockSpec((tr, tc), lambda i, c: (i, c)),
                  pl.BlockSpec((tr, tc), lambda i, c: (i, c + nc)),
                  pl.BlockSpec((tr, tc), lambda i, c: (i, c + 2 * nc)),
                  pl.BlockSpec((SUBLANES, tc), prev_map(0)),
                  pl.BlockSpec((SUBLANES, tc), prev_map(2 * nc)),
                  pl.BlockSpec((CONV_K, tc), lambda i, c: (0, c))],
        out_specs=pl.BlockSpec((tr, tc), lambda i, c: (i, c)),
        out_shape=jax.ShapeDtypeStruct((s, CONV_DIM), BF16),
        scratch_shapes=[pltpu.VMEM((tr + SUBLANES, tc), F32)],
        compiler_params=pltpu.CompilerParams(
            dimension_semantics=("parallel", "parallel"),
            vmem_limit_bytes=_vmem_limit(10 * tr * tc * 4)),
        name="gated_conv",
    )(za, za, za, za, za, conv_w)


def _t5_bucket_np(dist):
    n = np.maximum(dist, 0)
    max_exact = N_BUCKETS // 2
    nf = np.maximum(n, 1).astype(np.float64)
    large = max_exact + (np.log(nf / max_exact) / math.log(MAX_DISTANCE / max_exact)
                         * (N_BUCKETS - max_exact)).astype(np.int32)
    large = np.minimum(large, N_BUCKETS - 1)
    return np.where(n < max_exact, n, large).astype(np.int32)


def _near_bias(rel_bias):
    sl = np.arange(Q_TILE)[:, None]
    tl = np.arange(Q_TILE)[None, :]
    dist = np.stack([tl - sl + Q_TILE, tl - sl])
    assert _t5_bucket_np(np.array([Q_TILE]))[0] == N_BUCKETS - 1
    bucket = _t5_bucket_np(dist)
    rel = (rel_bias - rel_bias[N_BUCKETS - 1][None, :]) * (HEAD_DIM ** 0.5)
    onehot = np.eye(N_BUCKETS, dtype=np.float32)[bucket.reshape(-1)]
    b = jnp.dot(onehot, rel, precision=lax.Precision.HIGHEST)
    b = b.reshape(2, Q_TILE, Q_TILE, N_KV_HEADS, GROUP)
    b = b.transpose(0, 3, 1, 4, 2)
    b = b.reshape(2, N_KV_HEADS, Q_TILE, GROUP * Q_TILE).astype(F32)
    return jnp.concatenate([jnp.zeros_like(b[:1]), b], axis=0)


def _dsa_kernel(q_ref, k_ref, iq_lo_ref, iq_hi_ref, vt_ref, ik2_ref, w_ref, bias_ref, o_ref,
                sc_ref, s_ref, acc_ref, m_ref, *, topk, max_iters):
    i = pl.program_id(0)
    q0 = i * Q_TILE
    kf = float(topk)
    per = ATT_KEYS // Q_TILE
    c_last = i // per
    n_cnt = c_last + 1
    n_idx = (q0 + Q_TILE + IDX_KEYS - 1) // IDX_KEYS
    qpos = q0 + lax.broadcasted_iota(jnp.int32, (1, Q_TILE), 1)

    def fold8(x, op):
        rows = x.shape[0]
        if rows > SUBLANES * SUBLANES:
            x = op(x.reshape(SUBLANES, rows // SUBLANES, x.shape[1]), axis=0)
        return op(x.reshape(x.shape[0] // SUBLANES, SUBLANES, x.shape[1]), axis=0)

    def col_reduce(x, op):
        return op(fold8(x, op), axis=0, keepdims=True)

    n_dots = IDX_ALL // (2 * LANES)
    half = n_dots // 2

    def head_pairs(d):
        ref, e = (iq_lo_ref, d) if d < half else (iq_hi_ref, d - half)
        return jnp.concatenate([ref[:, (2 * e) * LANES:(2 * e + 1) * LANES],
                                ref[:, (2 * e + 1) * LANES:(2 * e + 2) * LANES]], axis=0)

    rhs = [head_pairs(d) for d in range(n_dots)]

    def idx_body(j, carry):
        mx, mn = carry
        s0 = pl.multiple_of(j * IDX_KEYS, IDX_KEYS)
        lhs = jnp.concatenate([ik2_ref[pl.ds(s0, IDX_KEYS), 0:LANES],
                               ik2_ref[pl.ds(s0, IDX_KEYS), LANES:2 * LANES]], axis=0)
        acc = None
        for d in range(n_dots):
            r = lax.dot_general(lhs, rhs[d], NT_DIMS, preferred_element_type=F32)
            r = jnp.maximum(r, 0.0)
            t = r[:IDX_KEYS] * w_ref[2 * d:2 * d + 1, :] + r[IDX_KEYS:] * w_ref[2 * d + 1:2 * d + 2, :]
            acc = t if acc is None else acc + t
        acc = acc[:, :Q_TILE] + acc[:, Q_TILE:]
        kpos = s0 + lax.broadcasted_iota(jnp.int32, (IDX_KEYS, 1), 0)
        causal = kpos <= qpos
        lo_fill = jnp.where(causal, acc, NEG_INF)
        sc_ref[pl.ds(s0, IDX_KEYS), :] = lo_fill
        mx = jnp.maximum(mx, fold8(lo_fill, jnp.max))
        mn = jnp.minimum(mn, fold8(jnp.where(causal, acc, -NEG_INF), jnp.min))
        return mx, mn

    mx8, mn8 = lax.fori_loop(
        0, n_idx, idx_body,
        (jnp.full((SUBLANES, Q_TILE), NEG_INF, F32), jnp.full((SUBLANES, Q_TILE), -NEG_INF, F32)))
    hi0 = jnp.max(mx8, axis=0, keepdims=True)
    lo0 = jnp.min(mn8, axis=0, keepdims=True)

    def fill_body(j, carry):
        sc_ref[pl.ds(pl.multiple_of(j * IDX_KEYS, IDX_KEYS), IDX_KEYS), :] = jnp.full(
            (IDX_KEYS, Q_TILE), NEG_INF, F32)
        return carry

    lax.fori_loop(n_idx, n_cnt * (ATT_KEYS // IDX_KEYS), fill_body, 0)

    def count_ge(tau):
        def body(c, cnt):
            blk = sc_ref[pl.ds(pl.multiple_of(c * ATT_KEYS, ATT_KEYS), ATT_KEYS), :]
            return cnt + fold8(jnp.where(blk >= tau, 1.0, 0.0), jnp.sum)
        c8 = lax.fori_loop(0, n_cnt, body, jnp.zeros((SUBLANES, Q_TILE), F32))
        return jnp.sum(c8, axis=0, keepdims=True)

    def n_active(lo, hi, flo):
        mid = 0.5 * lo + 0.5 * hi
        act = jnp.logical_and(flo > kf, jnp.logical_and(mid > lo, mid < hi))
        return act, jnp.sum(jnp.where(act, 1.0, 0.0))

    flo0 = (qpos + 1).astype(F32)

    def bis_cond(st):
        _, _, _, n, it = st
        return jnp.logical_and(n > 0.0, it < max_iters)

    def bis_body(st):
        lo, hi, flo, _, it = st
        act, _ = n_active(lo, hi, flo)
        mid = 0.5 * lo + 0.5 * hi
        c = count_ge(mid)
        up = jnp.logical_and(act, c >= kf)
        dn = jnp.logical_and(act, c < kf)
        lo = jnp.where(up, mid, lo)
        flo = jnp.where(up, c, flo)
        hi = jnp.where(dn, mid, hi)
        _, n = n_active(lo, hi, flo)
        return lo, hi, flo, n, it + 1

    _, n0 = n_active(lo0, hi0, flo0)
    lo, hi, flo, _, _ = lax.while_loop(bis_cond, bis_body, (lo0, hi0, flo0, n0, jnp.int32(0)))
    open_hi = jnp.logical_and(flo > kf, hi == hi0)
    n_open = jnp.sum(jnp.where(open_hi, 1.0, 0.0))
    c_hi = lax.cond(n_open > 0.0, lambda: count_ge(hi), lambda: jnp.zeros((1, Q_TILE), F32))
    tau = jnp.where(jnp.logical_and(open_hi, c_hi >= kf), hi, lo)

    c2 = (HEAD_DIM ** -0.5) * math.log2(math.e)
    qg = []
    for g in range(N_KV_HEADS):
        qg.append(jnp.concatenate(
            [q_ref[:, (g * GROUP + hh) * HEAD_DIM:(g * GROUP + hh + 1) * HEAD_DIM] for hh in range(GROUP)],
            axis=0))

    m_ref[...] = jnp.full(m_ref.shape, NEG_INF, F32)
    acc_ref[...] = jnp.zeros(acc_ref.shape, F32)

    def logits(c, g, slot):
        s0 = pl.multiple_of(jnp.minimum(c, n_cnt - 1) * ATT_KEYS, ATT_KEYS)
        kb = k_ref[pl.ds(s0, ATT_KEYS), g * HEAD_DIM:(g + 1) * HEAD_DIM]
        s_ref[slot] = lax.dot_general(kb, qg[g], NT_DIMS, preferred_element_type=F32)

    def chunk(c, near):
        s0 = pl.multiple_of(c * ATT_KEYS, ATT_KEYS)
        blk = sc_ref[pl.ds(s0, ATT_KEYS), :]
        madd = jnp.where(blk >= tau, 0.0, NEG_INF)
        madd = jnp.concatenate([madd] * GROUP, axis=1)
        for g in range(N_KV_HEADS):
            if g + 1 < N_KV_HEADS:
                logits(c, g + 1, (g + 1) % 2)
            else:
                logits(c + 1, 0, 0)
            vtb = vt_ref[g, :, pl.ds(s0, ATT_KEYS)]
            s = s_ref[g % 2] + madd
            if near:
                rows = []
                for r in range(per):
                    d = i - (c * per + r)
                    rows.append(bias_ref[jnp.where(d == 1, 1, jnp.where(d == 0, 2, 0)), g])
                s = s + jnp.concatenate(rows, axis=0)
            m_old = m_ref[g]
            m_new = jnp.maximum(m_old, col_reduce(s, jnp.max))
            m_use = jnp.where(m_new == NEG_INF, 0.0, m_new)
            alpha = jnp.exp2(c2 * (m_old - m_use))
            p = jnp.exp2(c2 * (s - m_use))
            pv = jnp.dot(vtb, p.astype(BF16), preferred_element_type=F32)
            acc_ref[g] = alpha * acc_ref[g] + pv
            m_ref[g] = m_new

    n_far = jnp.maximum(c_last - 1, 0)
    logits(0, 0, 0)

    def far_body(c, carry):
        chunk(c, False)
        return carry

    lax.fori_loop(0, n_far, far_body, 0)

    def near_body(c, carry):
        chunk(c, True)
        return carry

    lax.fori_loop(n_far, n_cnt, near_body, 0)

    for g in range(N_KV_HEADS):
        out = acc_ref[g, 0:HEAD_DIM, :] / acc_ref[g, HEAD_DIM:HEAD_DIM + 1, :]
        for hh in range(GROUP):
            h = g * GROUP + hh
            o_ref[:, h * HEAD_DIM:(h + 1) * HEAD_DIM] = (
                out[:, hh * Q_TILE:(hh + 1) * Q_TILE].T.astype(o_ref.dtype))


def _dsa_attention(zb, zc, rel_bias, topk):
    s = zb.shape[0]
    nq = s // Q_TILE
    assert s % ATT_KEYS == 0
    k_col = ATTN_DIM // KV_DIM
    iq_half = IDX_ALL // 2
    iq_col = (ATTN_DIM + 2 * KV_DIM) // iq_half
    assert (ATTN_DIM + 2 * KV_DIM) % iq_half == 0
    vt = zb[:, ATTN_DIM + KV_DIM:ATTN_DIM + 2 * KV_DIM].T.reshape(N_KV_HEADS, HEAD_DIM, s)
    vt = jnp.concatenate([vt, jnp.ones((N_KV_HEADS, V_ROWS - HEAD_DIM, s), BF16)], axis=1)
    ik2 = zc[:, :2 * LANES].astype(BF16)
    w = zc[:, 2 * LANES:2 * LANES + IDX_HEADS] * (IDX_HEADS ** -0.5) * (IDX_DIM ** -0.5)
    w = w.reshape(nq, Q_TILE, IDX_HEADS // 4, 2, 2).transpose(0, 2, 4, 3, 1)
    w = w.reshape(nq, IDX_HEADS // 2, 2 * Q_TILE)
    bias = _near_bias(rel_bias)

    resident = dict(pipeline_mode=pl.Buffered(1))
    nbytes = (s * KV_DIM * 2 + vt.size * 2 + s * 2 * LANES * 2 + bias.size * 4 + s * Q_TILE * 4
              + 2 * (2 * Q_TILE * ATTN_DIM * 2 * 2 + IDX_HEADS * Q_TILE * 4)
              + 6 * ATT_KEYS * GROUP * Q_TILE * 4 + 3 * N_KV_HEADS * V_ROWS * GROUP * Q_TILE * 4)
    return pl.pallas_call(
        functools.partial(_dsa_kernel, topk=topk, max_iters=400),
        grid=(nq,),
        in_specs=[pl.BlockSpec((Q_TILE, ATTN_DIM), lambda i: (i, 0)),
                  pl.BlockSpec((s, KV_DIM), lambda i: (0, k_col), **resident),
                  pl.BlockSpec((Q_TILE, iq_half), lambda i: (i, iq_col)),
                  pl.BlockSpec((Q_TILE, iq_half), lambda i: (i, iq_col + 1)),
                  pl.BlockSpec(vt.shape, lambda i: (0, 0, 0), **resident),
                  pl.BlockSpec((s, 2 * LANES), lambda i: (0, 0), **resident),
                  pl.BlockSpec((None, IDX_HEADS // 2, 2 * Q_TILE), lambda i: (i, 0, 0)),
                  pl.BlockSpec(bias.shape, lambda i: (0, 0, 0, 0), **resident)],
        out_specs=pl.BlockSpec((Q_TILE, ATTN_DIM), lambda i: (i, 0)),
        out_shape=jax.ShapeDtypeStruct((s, ATTN_DIM), BF16),
        scratch_shapes=[pltpu.VMEM((s, Q_TILE), F32),
                        pltpu.VMEM((2, ATT_KEYS, GROUP * Q_TILE), F32),
                        pltpu.VMEM((N_KV_HEADS, V_ROWS, GROUP * Q_TILE), F32),
                        pltpu.VMEM((N_KV_HEADS, 1, GROUP * Q_TILE), F32)],
        compiler_params=pltpu.CompilerParams(
            dimension_semantics=("arbitrary",),
            vmem_limit_bytes=_vmem_limit(nbytes)),
        name="dsa_attention",
    )(zb, zb, zb, zb, vt, ik2, w, bias)


def _merge_kernel(a_ref, o_ref, wa_ref, wo_ref, ga_ref, gb_ref, out_ref, wab_ref, wob_ref):
    @pl.when(pl.program_id(1) == 0)
    def _():
        wab_ref[...] = wa_ref[...].astype(BF16)
        wob_ref[...] = wo_ref[...].astype(BF16)

    ya = jnp.dot(a_ref[...], wab_ref[...], preferred_element_type=F32)
    yb = jnp.dot(o_ref[...], wob_ref[...], preferred_element_type=F32)
    merged = jax.nn.sigmoid(ga_ref[...]) * ya + jax.nn.sigmoid(gb_ref[...]) * yb
    out_ref[...] = merged.astype(out_ref.dtype)


def _merge(a_in, o, w_conv_out, w_attn_out, layer, zg, g_col0, tm=512, tn=512):
    s = a_in.shape[0]
    tm = min(tm, s)
    nn = D_MODEL // tn
    assert g_col0 % tn == 0
    gj = g_col0 // tn
    nbytes = (2 * (2 * tm * CONV_DIM * 2 + 2 * CONV_DIM * tn * 4 + 2 * tm * tn * 4 + tm * tn * 2)
              + 2 * CONV_DIM * tn * 2 + 3 * tm * tn * 4)
    return pl.pallas_call(
        _merge_kernel,
        grid=(nn, s // tm),
        in_specs=[pl.BlockSpec((tm, CONV_DIM), lambda j, i: (i, 0)),
                  pl.BlockSpec((tm, ATTN_DIM), lambda j, i: (i, 0)),
                  pl.BlockSpec((None, CONV_DIM, tn), lambda j, i: (layer, 0, j)),
                  pl.BlockSpec((None, ATTN_DIM, tn), lambda j, i: (layer, 0, j)),
                  pl.BlockSpec((tm, tn), lambda j, i: (i, j + gj)),
                  pl.BlockSpec((tm, tn), lambda j, i: (i, j + gj + nn))],
        out_specs=pl.BlockSpec((tm, tn), lambda j, i: (i, j)),
        out_shape=jax.ShapeDtypeStruct((s, D_MODEL), BF16),
        scratch_shapes=[pltpu.VMEM((CONV_DIM, tn), BF16), pltpu.VMEM((ATTN_DIM, tn), BF16)],
        compiler_params=pltpu.CompilerParams(
            dimension_semantics=("parallel", "arbitrary"),
            vmem_limit_bytes=_vmem_limit(nbytes)),
        name="merge",
    )(a_in, o, w_conv_out, w_attn_out, zg, zg)


def _xattn_kernel(x_ref, gx_ref, gm_ref, wq_ref, kx_ref, vx_ref, wo_ref, out_ref, h_ref, wqb_ref, wob_ref):
    @pl.when(pl.program_id(0) == 0)
    def _():
        wqb_ref[...] = wq_ref[...].astype(BF16)
        wob_ref[...] = wo_ref[...].astype(BF16)

    def rms(v, g_ref):
        inv = lax.rsqrt(jnp.mean(v * v, axis=-1, keepdims=True) + EPS)
        return ((v * inv) * g_ref[...]).astype(BF16)

    x = x_ref[...]
    qx = jnp.dot(rms(x, gx_ref), wqb_ref[...], preferred_element_type=F32).astype(BF16)
    scale = X_HEAD_DIM ** -0.5
    outs = []
    for h in range(X_HEADS):
        sl = slice(h * X_HEAD_DIM, (h + 1) * X_HEAD_DIM)
        s = lax.dot_general(qx[:, sl], kx_ref[:, sl], NT_DIMS, preferred_element_type=F32) * scale
        m = jnp.max(s, axis=-1, keepdims=True)
        p = jnp.exp(s - m)
        l = jnp.sum(p, axis=-1, keepdims=True)
        oh = jnp.dot(p.astype(BF16), vx_ref[:, sl], preferred_element_type=F32) / l
        outs.append(oh.astype(BF16))
    o = jnp.concatenate(outs, axis=1)
    x_new = x + jnp.dot(o, wob_ref[...], preferred_element_type=F32)
    out_ref[...] = x_new
    h_ref[...] = rms(x_new, gm_ref)


def _cross_attention(x, g_xattn, g_mlp, w_xq, kv, w_xo, layer, tm=256):
    s = x.shape[0]
    tm = min(tm, s)
    n_mem = kv.shape[0]
    resident = dict(pipeline_mode=pl.Buffered(1))
    nbytes = (2 * (2 * tm * D_MODEL * 4 + tm * D_MODEL * 2) + 2 * D_MODEL * X_DIM * (4 + 2)
              + 2 * n_mem * X_DIM * 2 + 4 * tm * D_MODEL * 4)
    return pl.pallas_call(
        _xattn_kernel,
        grid=(s // tm,),
        in_specs=[pl.BlockSpec((tm, D_MODEL), lambda i: (i, 0)),
                  pl.BlockSpec((1, D_MODEL), lambda i: (0, 0)),
                  pl.BlockSpec((1, D_MODEL), lambda i: (0, 0)),
                  pl.BlockSpec((None, D_MODEL, X_DIM), lambda i: (layer, 0, 0), **resident),
                  pl.BlockSpec((n_mem, X_DIM), lambda i: (0, 0), **resident),
                  pl.BlockSpec((n_mem, X_DIM), lambda i: (0, 1), **resident),
                  pl.BlockSpec((None, X_DIM, D_MODEL), lambda i: (layer, 0, 0), **resident)],
        out_specs=[pl.BlockSpec((tm, D_MODEL), lambda i: (i, 0)),
                   pl.BlockSpec((tm, D_MODEL), lambda i: (i, 0))],
        out_shape=[jax.ShapeDtypeStruct((s, D_MODEL), F32),
                   jax.ShapeDtypeStruct((s, D_MODEL), BF16)],
        scratch_shapes=[pltpu.VMEM((D_MODEL, X_DIM), BF16), pltpu.VMEM((X_DIM, D_MODEL), BF16)],
        compiler_params=pltpu.CompilerParams(
            dimension_semantics=("arbitrary",),
            vmem_limit_bytes=_vmem_limit(nbytes)),
        name="cross_attention",
    )(x, g_xattn.reshape(1, D_MODEL), g_mlp.reshape(1, D_MODEL), w_xq, kv, kv, w_xo)


CONV_END = 3 * CONV_DIM
ATTN_END = CONV_END + ATTN_DIM + 2 * KV_DIM + IDX_ALL
IK_END = ATTN_END + IDX_DIM
IW_END = IK_END + IDX_HEADS


def _tail_proj_weights(w):
    w_ik = w[:, ATTN_END:IK_END]
    zeros = jnp.zeros((w.shape[0], 2 * IDX_DIM), w.dtype)
    pad = jnp.zeros((w.shape[0], LANES - IDX_HEADS), w.dtype)
    wc = jnp.concatenate([w_ik, zeros, w_ik, w[:, IK_END:IW_END], pad], axis=1).astype(BF16)
    wg = w[:, IW_END:].astype(BF16)
    return wc, wg


def kernel(x, mem, rel_bias, norm_mix, w_in, conv_w, w_conv_out, w_attn_out, w_mix_out, norm_xattn, norm_mem, w_xq, w_xkv, w_xo, norm_mlp, w_up, w_down, norm_final):
    bsz, s, d = x.shape
    assert bsz == 1 and d == D_MODEL
    depth = w_in.shape[0]
    topk = min(TOPK_MAX, s // 4)
    xs = x.reshape(s, d)
    mems = mem.reshape(mem.shape[1], d)
    for l in range(depth):
        wc, wg = _tail_proj_weights(w_in[l])
        h = _rmsnorm(xs, norm_mix[l], BF16)
        za = _matmul(h, w_in, layer=l, col0=0, n=CONV_END, out_dtype=F32, tm=1024, tn=512, name="proj_conv")
        zb = _matmul(h, w_in, layer=l, col0=CONV_END, n=ATTN_END - CONV_END, out_dtype=BF16,
                     tm=1024, tn=512, name="proj_attn")
        zc = _matmul(h, wc, out_dtype=F32, tm=1024, tn=3 * LANES, name="proj_idx")
        zg = _matmul(h, wg, out_dtype=F32, tm=1024, tn=1024, name="proj_gate")
        a_in = _gated_conv(za, conv_w[l])
        o = _dsa_attention(zb, zc, rel_bias, topk)
        merged = _merge(a_in, o, w_conv_out, w_attn_out, l, zg, 0)
        xs = _matmul(merged, w_mix_out, layer=l, out_dtype=F32, tm=1024, tn=512,
                     epilogue="residual", residual=xs, name="mix_out")
        hm = _rmsnorm(mems, norm_mem[l], BF16)
        kv = _matmul(hm, w_xkv, layer=l, out_dtype=BF16, tm=256, tn=512, name="mem_kv")
        xs, hmlp = _cross_attention(xs, norm_xattn[l], norm_mlp[l], w_xq, kv, w_xo, l)
        act = _matmul(hmlp, w_up, layer=l, out_dtype=BF16, tm=1024, tn=512, epilogue="relu2", name="mlp_up")
        xs = _matmul(act, w_down, layer=l, out_dtype=F32, tm=1024, tn=512, tk=4096,
                     epilogue="residual", residual=xs, name="mlp_down")
    out = _rmsnorm(xs, norm_final, F32)
    return out.reshape(bsz, s, d)
```

```python
import functools
import math

import numpy as np
import jax
import jax.numpy as jnp
from jax import lax
from jax.experimental import pallas as pl
from jax.experimental.pallas import tpu as pltpu

F32 = jnp.float32
BF16 = jnp.bfloat16

D_MODEL = 4096
CONV_DIM = 2048
CONV_K = 3
N_HEADS = 16
N_KV_HEADS = 4
HEAD_DIM = 128
GROUP = N_HEADS // N_KV_HEADS
ATTN_DIM = N_HEADS * HEAD_DIM
KV_DIM = N_KV_HEADS * HEAD_DIM
IDX_HEADS = 32
IDX_DIM = 64
IDX_ALL = IDX_HEADS * IDX_DIM
TOPK_MAX = 256
N_BUCKETS = 32
MAX_DISTANCE = 128
X_HEADS = 4
X_HEAD_DIM = 128
X_DIM = X_HEADS * X_HEAD_DIM
EPS = 1e-6

V7X_VMEM_BYTES = 64 * 1024 * 1024
LANES = 128
SUBLANES = 8

Q_TILE = 128
IDX_KEYS = 256
ATT_KEYS = 512
V_ROWS = HEAD_DIM + 16
NEG_INF = float("-inf")
NT_DIMS = (((1,), (1,)), ((), ()))


def _vmem_limit(nbytes):
    return int(min(nbytes + (8 << 20), V7X_VMEM_BYTES - (6 << 20)))


def _rmsnorm_kernel(x_ref, g_ref, o_ref):
    x = x_ref[...]
    inv = lax.rsqrt(jnp.mean(x * x, axis=-1, keepdims=True) + EPS)
    o_ref[...] = ((x * inv) * g_ref[...]).astype(o_ref.dtype)


def _rmsnorm(x, g, out_dtype, tr=256):
    rows, d = x.shape
    tr = min(tr, rows)
    return pl.pallas_call(
        _rmsnorm_kernel,
        grid=(rows // tr,),
        in_specs=[pl.BlockSpec((tr, d), lambda i: (i, 0)),
                  pl.BlockSpec((1, d), lambda i: (0, 0))],
        out_specs=pl.BlockSpec((tr, d), lambda i: (i, 0)),
        out_shape=jax.ShapeDtypeStruct((rows, d), out_dtype),
        compiler_params=pltpu.CompilerParams(
            dimension_semantics=("parallel",),
            vmem_limit_bytes=_vmem_limit(2 * tr * d * (4 + jnp.dtype(out_dtype).itemsize))),
        name="rmsnorm",
    )(x, g.reshape(1, d))


def _mm_kernel(*refs, nk, epilogue, b_transposed):
    if epilogue == "residual":
        a_ref, b_ref, r_ref, o_ref = refs[:4]
        scratch = refs[4:]
    else:
        a_ref, b_ref, o_ref = refs[:3]
        r_ref = None
        scratch = refs[3:]

    def finish(acc):
        if epilogue == "relu2":
            acc = jnp.square(jnp.maximum(acc, 0.0))
        elif epilogue == "residual":
            acc = r_ref[...] + acc
        o_ref[...] = acc.astype(o_ref.dtype)

    if b_transposed:
        d = lax.dot_general(a_ref[...], b_ref[0].astype(BF16), NT_DIMS, preferred_element_type=F32)
    else:
        d = jnp.dot(a_ref[...], b_ref[...].astype(BF16), preferred_element_type=F32)
    if nk == 1:
        finish(d)
        return
    acc_ref, = scratch
    k = pl.program_id(2)

    @pl.when(k == 0)
    def _():
        acc_ref[...] = d

    @pl.when(jnp.logical_and(k > 0, k < nk - 1))
    def _():
        acc_ref[...] += d

    @pl.when(k == nk - 1)
    def _():
        finish(acc_ref[...] + d)


def _matmul(a, b, *, out_dtype, tm, tn, tk=None, layer=None, col0=0, n=None, b_transposed=False,
            epilogue=None, residual=None, name="matmul"):
    m, kdim = a.shape
    n_all = b.shape[-2] if b_transposed else b.shape[-1]
    n = (n_all - col0) if n is None else n
    tm, tn = min(tm, m), min(tn, n)
    tk = kdim if tk is None else tk
    nk = kdim // tk
    assert m % tm == 0 and n % tn == 0 and kdim % tk == 0
    if b_transposed:
        assert layer is not None and col0 % SUBLANES == 0
        b_spec = pl.BlockSpec((pl.Element(1), pl.Element(tn), pl.Element(tk)),
                              lambda i, j, k: (layer, pl.multiple_of(col0 + j * tn, SUBLANES), k * tk))
    else:
        assert col0 % tn == 0
        j0 = col0 // tn
        if layer is None:
            b_spec = pl.BlockSpec((tk, tn), lambda i, j, k: (k, j + j0))
        else:
            b_spec = pl.BlockSpec((None, tk, tn), lambda i, j, k: (layer, k, j + j0))
    in_specs = [pl.BlockSpec((tm, tk), lambda i, j, k: (i, k)), b_spec]
    args = [a, b]
    b_item = jnp.dtype(b.dtype).itemsize
    nbytes = 2 * tm * tk * 2 + 2 * tk * tn * b_item + 2 * tm * tn * jnp.dtype(out_dtype).itemsize
    if b_item != 2:
        nbytes += tk * tn * 2
    if epilogue == "residual":
        in_specs.append(pl.BlockSpec((tm, tn), lambda i, j, k: (i, j)))
        args.append(residual)
        nbytes += 2 * tm * tn * 4
    scratch = []
    if nk > 1:
        scratch.append(pltpu.VMEM((tm, tn), F32))
        nbytes += tm * tn * 4
    nbytes += tm * tn * 4
    return pl.pallas_call(
        functools.partial(_mm_kernel, nk=nk, epilogue=epilogue, b_transposed=b_transposed),
        grid=(m // tm, n // tn, nk),
        in_specs=in_specs,
        out_specs=pl.BlockSpec((tm, tn), lambda i, j, k: (i, j)),
        out_shape=jax.ShapeDtypeStruct((m, n), out_dtype),
        scratch_shapes=scratch,
        compiler_params=pltpu.CompilerParams(
            dimension_semantics=("parallel", "parallel", "arbitrary"),
            vmem_limit_bytes=_vmem_limit(nbytes)),
        name=name,
    )(*args)


def _conv_kernel(u_ref, cb_ref, cc_ref, up_ref, ccp_ref, w_ref, o_ref, p_ref, *, tr):
    i = pl.program_id(0)
    p = cc_ref[...] * u_ref[...]
    prev = jnp.where(i > 0, ccp_ref[...] * up_ref[...], 0.0)
    p_ref[0:SUBLANES, :] = prev
    p_ref[SUBLANES:SUBLANES + tr, :] = p
    p1 = p_ref[SUBLANES - 1:SUBLANES - 1 + tr, :]
    p2 = p_ref[SUBLANES - 2:SUBLANES - 2 + tr, :]
    w = w_ref[...]
    z = w[2:3, :] * p + w[0:1, :] * p2 + w[1:2, :] * p1
    o_ref[...] = (cb_ref[...] * z).astype(o_ref.dtype)


def _gated_conv(za, conv_w, tr=512, tc=512):
    s = za.shape[0]
    tr = min(tr, s)
    nc = CONV_DIM // tc
    rb = tr // SUBLANES

    def prev_map(off):
        return lambda i, c: (jnp.maximum(i * rb - 1, 0), c + off)

    return pl.pallas_call(
        functools.partial(_conv_kernel, tr=tr),
        grid=(s // tr, nc),
        in_specs=[pl.BlockSpec((tr, tc), lambda i, c: (i, c)),
                  pl.BlockSpec((tr, tc), lambda i, c: (i, c + nc)),
                  pl.BlockSpec((tr, tc), lambda i, c: (i, c + 2 * nc)),
                  pl.BlockSpec((SUBLANES, tc), prev_map(0)),
                  pl.BlockSpec((SUBLANES, tc), prev_map(2 * nc)),
                  pl.BlockSpec((CONV_K, tc), lambda i, c: (0, c))],
        out_specs=pl.BlockSpec((tr, tc), lambda i, c: (i, c)),
        out_shape=jax.ShapeDtypeStruct((s, CONV_DIM), BF16),
        scratch_shapes=[pltpu.VMEM((tr + SUBLANES, tc), F32)],
        compiler_params=pltpu.CompilerParams(
            dimension_semantics=("parallel", "parallel"),
            vmem_limit_bytes=_vmem_limit(10 * tr * tc * 4)),
        name="gated_conv",
    )(za, za, za, za, za, conv_w)


def _t5_bucket_np(dist):
    n = np.maximum(dist, 0)
    max_exact = N_BUCKETS // 2
    nf = np.maximum(n, 1).astype(np.float64)
    large = max_exact + (np.log(nf / max_exact) / math.log(MAX_DISTANCE / max_exact)
                         * (N_BUCKETS - max_exact)).astype(np.int32)
    large = np.minimum(large, N_BUCKETS - 1)
    return np.where(n < max_exact, n, large).astype(np.int32)


def _near_bias(rel_bias):
    sl = np.arange(Q_TILE)[:, None]
    tl = np.arange(Q_TILE)[None, :]
    dist = np.stack([tl - sl + Q_TILE, tl - sl])
    assert _t5_bucket_np(np.array([Q_TILE]))[0] == N_BUCKETS - 1
    bucket = _t5_bucket_np(dist)
    rel = (rel_bias - rel_bias[N_BUCKETS - 1][None, :]) * (HEAD_DIM ** 0.5)
    onehot = np.eye(N_BUCKETS, dtype=np.float32)[bucket.reshape(-1)]
    b = jnp.dot(onehot, rel, precision=lax.Precision.HIGHEST)
    b = b.reshape(2, Q_TILE, Q_TILE, N_KV_HEADS, GROUP)
    b = b.transpose(0, 3, 1, 4, 2)
    b = b.reshape(2, N_KV_HEADS, Q_TILE, GROUP * Q_TILE).astype(F32)
    return jnp.concatenate([jnp.zeros_like(b[:1]), b], axis=0)


def _dsa_kernel(q_ref, k_ref, iq_lo_ref, iq_hi_ref, vt_ref, ik2_ref, w_ref, bias_ref, o_ref,
                sc_ref, s_ref, acc_ref, m_ref, *, topk, max_iters):
    i = pl.program_id(0)
    q0 = i * Q_TILE
    kf = float(topk)
    per = ATT_KEYS // Q_TILE
    c_last = i // per
    n_cnt = c_last + 1
    n_idx = (q0 + Q_TILE + IDX_KEYS - 1) // IDX_KEYS
    qpos = q0 + lax.broadcasted_iota(jnp.int32, (1, Q_TILE), 1)

    def fold8(x, op):
        rows = x.shape[0]
        if rows > SUBLANES * SUBLANES:
            x = op(x.reshape(SUBLANES, rows // SUBLANES, x.shape[1]), axis=0)
        return op(x.reshape(x.shape[0] // SUBLANES, SUBLANES, x.shape[1]), axis=0)

    def col_reduce(x, op):
        return op(fold8(x, op), axis=0, keepdims=True)

    n_dots = IDX_ALL // (2 * LANES)
    half = n_dots // 2

    def head_pairs(d):
        ref, e = (iq_lo_ref, d) if d < half else (iq_hi_ref, d - half)
        return jnp.concatenate([ref[:, (2 * e) * LANES:(2 * e + 1) * LANES],
                                ref[:, (2 * e + 1) * LANES:(2 * e + 2) * LANES]], axis=0)

    rhs = [head_pairs(d) for d in range(n_dots)]

    def idx_body(j, carry):
        mx, mn = carry
        s0 = pl.multiple_of(j * IDX_KEYS, IDX_KEYS)
        lhs = jnp.concatenate([ik2_ref[pl.ds(s0, IDX_KEYS), 0:LANES],
                               ik2_ref[pl.ds(s0, IDX_KEYS), LANES:2 * LANES]], axis=0)
        acc = None
        for d in range(n_dots):
            r = lax.dot_general(lhs, rhs[d], NT_DIMS, preferred_element_type=F32)
            r = jnp.maximum(r, 0.0)
            t = r[:IDX_KEYS] * w_ref[2 * d:2 * d + 1, :] + r[IDX_KEYS:] * w_ref[2 * d + 1:2 * d + 2, :]
            acc = t if acc is None else acc + t
        acc = acc[:, :Q_TILE] + acc[:, Q_TILE:]
        kpos = s0 + lax.broadcasted_iota(jnp.int32, (IDX_KEYS, 1), 0)
        causal = kpos <= qpos
        lo_fill = jnp.where(causal, acc, NEG_INF)
        sc_ref[pl.ds(s0, IDX_KEYS), :] = lo_fill
        mx = jnp.maximum(mx, fold8(lo_fill, jnp.max))
        mn = jnp.minimum(mn, fold8(jnp.where(causal, acc, -NEG_INF), jnp.min))
        return mx, mn

    mx8, mn8 = lax.fori_loop(
        0, n_idx, idx_body,
        (jnp.full((SUBLANES, Q_TILE), NEG_INF, F32), jnp.full((SUBLANES, Q_TILE), -NEG_INF, F32)))
    hi0 = jnp.max(mx8, axis=0, keepdims=True)
    lo0 = jnp.min(mn8, axis=0, keepdims=True)

    def fill_body(j, carry):
        sc_ref[pl.ds(pl.multiple_of(j * IDX_KEYS, IDX_KEYS), IDX_KEYS), :] = jnp.full(
            (IDX_KEYS, Q_TILE), NEG_INF, F32)
        return carry

    lax.fori_loop(n_idx, n_cnt * (ATT_KEYS // IDX_KEYS), fill_body, 0)

    def count_ge(tau):
        def body(c, cnt):
            blk = sc_ref[pl.ds(pl.multiple_of(c * ATT_KEYS, ATT_KEYS), ATT_KEYS), :]
            return cnt + fold8(jnp.where(blk >= tau, 1.0, 0.0), jnp.sum)
        c8 = lax.fori_loop(0, n_cnt, body, jnp.zeros((SUBLANES, Q_TILE), F32))
        return jnp.sum(c8, axis=0, keepdims=True)

    def n_active(lo, hi, flo):
        mid = 0.5 * lo + 0.5 * hi
        act = jnp.logical_and(flo > kf, jnp.logical_and(mid > lo, mid < hi))
        return act, jnp.sum(jnp.where(act, 1.0, 0.0))

    flo0 = (qpos + 1).astype(F32)

    def bis_cond(st):
        _, _, _, n, it = st
        return jnp.logical_and(n > 0.0, it < max_iters)

    def bis_body(st):
        lo, hi, flo, _, it = st
        act, _ = n_active(lo, hi, flo)
        mid = 0.5 * lo + 0.5 * hi
        c = count_ge(mid)
        up = jnp.logical_and(act, c >= kf)
        dn = jnp.logical_and(act, c < kf)
        lo = jnp.where(up, mid, lo)
        flo = jnp.where(up, c, flo)
        hi = jnp.where(dn, mid, hi)
        _, n = n_active(lo, hi, flo)
        return lo, hi, flo, n, it + 1

    _, n0 = n_active(lo0, hi0, flo0)
    lo, hi, flo, _, _ = lax.while_loop(bis_cond, bis_body, (lo0, hi0, flo0, n0, jnp.int32(0)))
    open_hi = jnp.logical_and(flo > kf, hi == hi0)
    n_open = jnp.sum(jnp.where(open_hi, 1.0, 0.0))
    c_hi = lax.cond(n_open > 0.0, lambda: count_ge(hi), lambda: jnp.zeros((1, Q_TILE), F32))
    tau = jnp.where(jnp.logical_and(open_hi, c_hi >= kf), hi, lo)

    c2 = (HEAD_DIM ** -0.5) * math.log2(math.e)
    qg = []
    for g in range(N_KV_HEADS):
        qg.append(jnp.concatenate(
            [q_ref[:, (g * GROUP + hh) * HEAD_DIM:(g * GROUP + hh + 1) * HEAD_DIM] for hh in range(GROUP)],
            axis=0))

    m_ref[...] = jnp.full(m_ref.shape, NEG_INF, F32)
    acc_ref[...] = jnp.zeros(acc_ref.shape, F32)

    def logits(c, g, slot):
        s0 = pl.multiple_of(jnp.minimum(c, n_cnt - 1) * ATT_KEYS, ATT_KEYS)
        kb = k_ref[pl.ds(s0, ATT_KEYS), g * HEAD_DIM:(g + 1) * HEAD_DIM]
        s_ref[slot] = lax.dot_general(kb, qg[g], NT_DIMS, preferred_element_type=F32)

    def chunk(c, near):
        s0 = pl.multiple_of(c * ATT_KEYS, ATT_KEYS)
        blk = sc_ref[pl.ds(s0, ATT_KEYS), :]
        madd = jnp.where(blk >= tau, 0.0, NEG_INF)
        madd = jnp.concatenate([madd] * GROUP, axis=1)
        for g in range(N_KV_HEADS):
            if g + 1 < N_KV_HEADS:
                logits(c, g + 1, (g + 1) % 2)
            else:
                logits(c + 1, 0, 0)
            vtb = vt_ref[g, :, pl.ds(s0, ATT_KEYS)]
            s = s_ref[g % 2] + madd
            if near:
                rows = []
                for r in range(per):
                    d = i - (c * per + r)
                    rows.append(bias_ref[jnp.where(d == 1, 1, jnp.where(d == 0, 2, 0)), g])
                s = s + jnp.concatenate(rows, axis=0)
            m_old = m_ref[g]
            m_new = jnp.maximum(m_old, col_reduce(s, jnp.max))
            m_use = jnp.where(m_new == NEG_INF, 0.0, m_new)
            alpha = jnp.exp2(c2 * (m_old - m_use))
            p = jnp.exp2(c2 * (s - m_use))
            pv = jnp.dot(vtb, p.astype(BF16), preferred_element_type=F32)
            acc_ref[g] = alpha * acc_ref[g] + pv
            m_ref[g] = m_new

    n_far = jnp.maximum(c_last - 1, 0)
    logits(0, 0, 0)

    def far_body(c, carry):
        chunk(c, False)
        return carry

    lax.fori_loop(0, n_far, far_body, 0)

    def near_body(c, carry):
        chunk(c, True)
        return carry

    lax.fori_loop(n_far, n_cnt, near_body, 0)

    for g in range(N_KV_HEADS):
        out = acc_ref[g, 0:HEAD_DIM, :] / acc_ref[g, HEAD_DIM:HEAD_DIM + 1, :]
        for hh in range(GROUP):
            h = g * GROUP + hh
            o_ref[:, h * HEAD_DIM:(h + 1) * HEAD_DIM] = (
                out[:, hh * Q_TILE:(hh + 1) * Q_TILE].T.astype(o_ref.dtype))


def _dsa_attention(zb, zc, rel_bias, topk):
    s = zb.shape[0]
    nq = s // Q_TILE
    assert s % ATT_KEYS == 0
    k_col = ATTN_DIM // KV_DIM
    iq_half = IDX_ALL // 2
    iq_col = (ATTN_DIM + 2 * KV_DIM) // iq_half
    assert (ATTN_DIM + 2 * KV_DIM) % iq_half == 0
    vt = zb[:, ATTN_DIM + KV_DIM:ATTN_DIM + 2 * KV_DIM].T.reshape(N_KV_HEADS, HEAD_DIM, s)
    vt = jnp.concatenate([vt, jnp.ones((N_KV_HEADS, V_ROWS - HEAD_DIM, s), BF16)], axis=1)
    ik = zc[:, :IDX_DIM].astype(BF16)
    ik2 = jnp.concatenate([ik, jnp.zeros((s, 2 * IDX_DIM), BF16), ik], axis=1)
    w = zc[:, IDX_DIM:IDX_DIM + IDX_HEADS] * (IDX_HEADS ** -0.5) * (IDX_DIM ** -0.5)
    w = w.reshape(nq, Q_TILE, IDX_HEADS // 4, 2, 2).transpose(0, 2, 4, 3, 1)
    w = w.reshape(nq, IDX_HEADS // 2, 2 * Q_TILE)
    bias = _near_bias(rel_bias)

    resident = dict(pipeline_mode=pl.Buffered(1))
    nbytes = (s * KV_DIM * 2 + vt.size * 2 + s * 2 * LANES * 2 + bias.size * 4 + s * Q_TILE * 4
              + 2 * (2 * Q_TILE * ATTN_DIM * 2 * 2 + IDX_HEADS * Q_TILE * 4)
              + 6 * ATT_KEYS * GROUP * Q_TILE * 4 + 3 * N_KV_HEADS * V_ROWS * GROUP * Q_TILE * 4)
    return pl.pallas_call(
        functools.partial(_dsa_kernel, topk=topk, max_iters=400),
        grid=(nq,),
        in_specs=[pl.BlockSpec((Q_TILE, ATTN_DIM), lambda i: (i, 0)),
                  pl.BlockSpec((s, KV_DIM), lambda i: (0, k_col), **resident),
                  pl.BlockSpec((Q_TILE, iq_half), lambda i: (i, iq_col)),
                  pl.BlockSpec((Q_TILE, iq_half), lambda i: (i, iq_col + 1)),
                  pl.BlockSpec(vt.shape, lambda i: (0, 0, 0), **resident),
                  pl.BlockSpec((s, 2 * LANES), lambda i: (0, 0), **resident),
                  pl.BlockSpec((None, IDX_HEADS // 2, 2 * Q_TILE), lambda i: (i, 0, 0)),
                  pl.BlockSpec(bias.shape, lambda i: (0, 0, 0, 0), **resident)],
        out_specs=pl.BlockSpec((Q_TILE, ATTN_DIM), lambda i: (i, 0)),
        out_shape=jax.ShapeDtypeStruct((s, ATTN_DIM), BF16),
        scratch_shapes=[pltpu.VMEM((s, Q_TILE), F32),
                        pltpu.VMEM((2, ATT_KEYS, GROUP * Q_TILE), F32),
                        pltpu.VMEM((N_KV_HEADS, V_ROWS, GROUP * Q_TILE), F32),
                        pltpu.VMEM((N_KV_HEADS, 1, GROUP * Q_TILE), F32)],
        compiler_params=pltpu.CompilerParams(
            dimension_semantics=("arbitrary",),
            vmem_limit_bytes=_vmem_limit(nbytes)),
        name="dsa_attention",
    )(zb, zb, zb, zb, vt, ik2, w, bias)


def _merge_kernel(a_ref, o_ref, wa_ref, wo_ref, ga_ref, gb_ref, out_ref, wab_ref, wob_ref):
    @pl.when(pl.program_id(1) == 0)
    def _():
        wab_ref[...] = wa_ref[...].astype(BF16)
        wob_ref[...] = wo_ref[...].astype(BF16)

    ya = jnp.dot(a_ref[...], wab_ref[...], preferred_element_type=F32)
    yb = jnp.dot(o_ref[...], wob_ref[...], preferred_element_type=F32)
    merged = jax.nn.sigmoid(ga_ref[...]) * ya + jax.nn.sigmoid(gb_ref[...]) * yb
    out_ref[...] = merged.astype(out_ref.dtype)


def _merge(a_in, o, w_conv_out, w_attn_out, layer, zg, g_col0, tm=512, tn=512):
    s = a_in.shape[0]
    tm = min(tm, s)
    nn = D_MODEL // tn
    assert g_col0 % tn == 0
    gj = g_col0 // tn
    nbytes = (2 * (2 * tm * CONV_DIM * 2 + 2 * CONV_DIM * tn * 4 + 2 * tm * tn * 4 + tm * tn * 2)
              + 2 * CONV_DIM * tn * 2 + 3 * tm * tn * 4)
    return pl.pallas_call(
        _merge_kernel,
        grid=(nn, s // tm),
        in_specs=[pl.BlockSpec((tm, CONV_DIM), lambda j, i: (i, 0)),
                  pl.BlockSpec((tm, ATTN_DIM), lambda j, i: (i, 0)),
                  pl.BlockSpec((None, CONV_DIM, tn), lambda j, i: (layer, 0, j)),
                  pl.BlockSpec((None, ATTN_DIM, tn), lambda j, i: (layer, 0, j)),
                  pl.BlockSpec((tm, tn), lambda j, i: (i, j + gj)),
                  pl.BlockSpec((tm, tn), lambda j, i: (i, j + gj + nn))],
        out_specs=pl.BlockSpec((tm, tn), lambda j, i: (i, j)),
        out_shape=jax.ShapeDtypeStruct((s, D_MODEL), BF16),
        scratch_shapes=[pltpu.VMEM((CONV_DIM, tn), BF16), pltpu.VMEM((ATTN_DIM, tn), BF16)],
        compiler_params=pltpu.CompilerParams(
            dimension_semantics=("parallel", "arbitrary"),
            vmem_limit_bytes=_vmem_limit(nbytes)),
        name="merge",
    )(a_in, o, w_conv_out, w_attn_out, zg, zg)


def _xattn_kernel(x_ref, gx_ref, gm_ref, wq_ref, kx_ref, vx_ref, wo_ref, out_ref, h_ref, wqb_ref, wob_ref):
    @pl.when(pl.program_id(0) == 0)
    def _():
        wqb_ref[...] = wq_ref[...].astype(BF16)
        wob_ref[...] = wo_ref[...].astype(BF16)

    def rms(v, g_ref):
        inv = lax.rsqrt(jnp.mean(v * v, axis=-1, keepdims=True) + EPS)
        return ((v * inv) * g_ref[...]).astype(BF16)

    x = x_ref[...]
    qx = jnp.dot(rms(x, gx_ref), wqb_ref[...], preferred_element_type=F32).astype(BF16)
    scale = X_HEAD_DIM ** -0.5
    outs = []
    for h in range(X_HEADS):
        sl = slice(h * X_HEAD_DIM, (h + 1) * X_HEAD_DIM)
        s = lax.dot_general(qx[:, sl], kx_ref[:, sl], NT_DIMS, preferred_element_type=F32) * scale
        m = jnp.max(s, axis=-1, keepdims=True)
        p = jnp.exp(s - m)
        l = jnp.sum(p, axis=-1, keepdims=True)
        oh = jnp.dot(p.astype(BF16), vx_ref[:, sl], preferred_element_type=F32) / l
        outs.append(oh.astype(BF16))
    o = jnp.concatenate(outs, axis=1)
    x_new = x + jnp.dot(o, wob_ref[...], preferred_element_type=F32)
    out_ref[...] = x_new
    h_ref[...] = rms(x_new, gm_ref)


def _cross_attention(x, g_xattn, g_mlp, w_xq, kv, w_xo, layer, tm=256):
    s = x.shape[0]
    tm = min(tm, s)
    n_mem = kv.shape[0]
    resident = dict(pipeline_mode=pl.Buffered(1))
    nbytes = (2 * (2 * tm * D_MODEL * 4 + tm * D_MODEL * 2) + 2 * D_MODEL * X_DIM * (4 + 2)
              + 2 * n_mem * X_DIM * 2 + 4 * tm * D_MODEL * 4)
    return pl.pallas_call(
        _xattn_kernel,
        grid=(s // tm,),
        in_specs=[pl.BlockSpec((tm, D_MODEL), lambda i: (i, 0)),
                  pl.BlockSpec((1, D_MODEL), lambda i: (0, 0)),
                  pl.BlockSpec((1, D_MODEL), lambda i: (0, 0)),
                  pl.BlockSpec((None, D_MODEL, X_DIM), lambda i: (layer, 0, 0), **resident),
                  pl.BlockSpec((n_mem, X_DIM), lambda i: (0, 0), **resident),
                  pl.BlockSpec((n_mem, X_DIM), lambda i: (0, 1), **resident),
                  pl.BlockSpec((None, X_DIM, D_MODEL), lambda i: (layer, 0, 0), **resident)],
        out_specs=[pl.BlockSpec((tm, D_MODEL), lambda i: (i, 0)),
                   pl.BlockSpec((tm, D_MODEL), lambda i: (i, 0))],
        out_shape=[jax.ShapeDtypeStruct((s, D_MODEL), F32),
                   jax.ShapeDtypeStruct((s, D_MODEL), BF16)],
        scratch_shapes=[pltpu.VMEM((D_MODEL, X_DIM), BF16), pltpu.VMEM((X_DIM, D_MODEL), BF16)],
        compiler_params=pltpu.CompilerParams(
            dimension_semantics=("arbitrary",),
            vmem_limit_bytes=_vmem_limit(nbytes)),
        name="cross_attention",
    )(x, g_xattn.reshape(1, D_MODEL), g_mlp.reshape(1, D_MODEL), w_xq, kv, kv, w_xo)


CONV_END = 3 * CONV_DIM
ATTN_END = CONV_END + ATTN_DIM + 2 * KV_DIM + IDX_ALL
IW_END = ATTN_END + IDX_DIM + IDX_HEADS


def kernel(x, mem, rel_bias, norm_mix, w_in, conv_w, w_conv_out, w_attn_out, w_mix_out, norm_xattn, norm_mem, w_xq, w_xkv, w_xo, norm_mlp, w_up, w_down, norm_final):
    bsz, s, d = x.shape
    assert bsz == 1 and d == D_MODEL
    depth = w_in.shape[0]
    topk = min(TOPK_MAX, s // 4)
    xs = x.reshape(s, d)
    mems = mem.reshape(mem.shape[1], d)
    w_in_t = jnp.swapaxes(w_in, 1, 2)
    for l in range(depth):
        h = _rmsnorm(xs, norm_mix[l], BF16)
        proj = functools.partial(_matmul, h, w_in_t, layer=l, b_transposed=True, tm=1024)
        za = proj(col0=0, n=CONV_END, out_dtype=F32, tn=512, name="proj_conv")
        zb = proj(col0=CONV_END, n=ATTN_END - CONV_END, out_dtype=BF16, tn=512, name="proj_attn")
        zc = proj(col0=ATTN_END, n=LANES, out_dtype=F32, tn=LANES, name="proj_idx")
        zg = proj(col0=IW_END, n=2 * D_MODEL, out_dtype=F32, tn=512, name="proj_gate")
        a_in = _gated_conv(za, conv_w[l])
        o = _dsa_attention(zb, zc, rel_bias, topk)
        merged = _merge(a_in, o, w_conv_out, w_attn_out, l, zg, 0)
        xs = _matmul(merged, w_mix_out, layer=l, out_dtype=F32, tm=1024, tn=512,
                     epilogue="residual", residual=xs, name="mix_out")
        hm = _rmsnorm(mems, norm_mem[l], BF16)
        kv = _matmul(hm, w_xkv, layer=l, out_dtype=BF16, tm=256, tn=512, name="mem_kv")
        xs, hmlp = _cross_attention(xs, norm_xattn[l], norm_mlp[l], w_xq, kv, w_xo, l)
        act = _matmul(hmlp, w_up, layer=l, out_dtype=BF16, tm=1024, tn=512, epilogue="relu2", name="mlp_up")
        xs = _matmul(act, w_down, layer=l, out_dtype=F32, tm=1024, tn=512, tk=4096,
                     epilogue="residual", residual=xs, name="mlp_down")
    out = _rmsnorm(xs, norm_final, F32)
    return out.reshape(bsz, s, d)
```

```python
import functools
import math

import numpy as np
import jax
import jax.numpy as jnp
from jax import lax
from jax.experimental import pallas as pl
from jax.experimental.pallas import tpu as pltpu

F32 = jnp.float32
BF16 = jnp.bfloat16

D_MODEL = 4096
CONV_DIM = 2048
CONV_K = 3
N_HEADS = 16
N_KV_HEADS = 4
HEAD_DIM = 128
GROUP = N_HEADS // N_KV_HEADS
ATTN_DIM = N_HEADS * HEAD_DIM
KV_DIM = N_KV_HEADS * HEAD_DIM
IDX_HEADS = 32
IDX_DIM = 64
IDX_ALL = IDX_HEADS * IDX_DIM
TOPK_MAX = 256
N_BUCKETS = 32
MAX_DISTANCE = 128
X_HEADS = 4
X_HEAD_DIM = 128
X_DIM = X_HEADS * X_HEAD_DIM
EPS = 1e-6

V7X_VMEM_BYTES = 64 * 1024 * 1024
LANES = 128
SUBLANES = 8

Q_TILE = 128
IDX_KEYS = 512
ATT_KEYS = 512
V_ROWS = HEAD_DIM + 16
BISECT_STEPS = 4
NEG_INF = float("-inf")
NT_DIMS = (((1,), (1,)), ((), ()))


def _vmem_limit(nbytes):
    return int(min(nbytes + (8 << 20), V7X_VMEM_BYTES - (6 << 20)))


def _rmsnorm_kernel(x_ref, g_ref, o_ref):
    x = x_ref[...]
    inv = lax.rsqrt(jnp.mean(x * x, axis=-1, keepdims=True) + EPS)
    o_ref[...] = ((x * inv) * g_ref[...]).astype(o_ref.dtype)


def _rmsnorm(x, g, out_dtype, tr=256):
    rows, d = x.shape
    tr = min(tr, rows)
    return pl.pallas_call(
        _rmsnorm_kernel,
        grid=(rows // tr,),
        in_specs=[pl.BlockSpec((tr, d), lambda i: (i, 0)),
                  pl.BlockSpec((1, d), lambda i: (0, 0))],
        out_specs=pl.BlockSpec((tr, d), lambda i: (i, 0)),
        out_shape=jax.ShapeDtypeStruct((rows, d), out_dtype),
        compiler_params=pltpu.CompilerParams(
            dimension_semantics=("parallel",),
            vmem_limit_bytes=_vmem_limit(2 * tr * d * (4 + jnp.dtype(out_dtype).itemsize))),
        name="rmsnorm",
    )(x, g.reshape(1, d))


def _mm_kernel(*refs, nk, epilogue, b_transposed):
    if epilogue == "residual":
        a_ref, b_ref, r_ref, o_ref = refs[:4]
        scratch = refs[4:]
    else:
        a_ref, b_ref, o_ref = refs[:3]
        r_ref = None
        scratch = refs[3:]

    def finish(acc):
        if epilogue == "relu2":
            acc = jnp.square(jnp.maximum(acc, 0.0))
        elif epilogue == "residual":
            acc = r_ref[...] + acc
        o_ref[...] = acc.astype(o_ref.dtype)

    if b_transposed:
        d = lax.dot_general(a_ref[...], b_ref[0].astype(BF16), NT_DIMS, preferred_element_type=F32)
    else:
        d = jnp.dot(a_ref[...], b_ref[...].astype(BF16), preferred_element_type=F32)
    if nk == 1:
        finish(d)
        return
    acc_ref, = scratch
    k = pl.program_id(2)

    @pl.when(k == 0)
    def _():
        acc_ref[...] = d

    @pl.when(jnp.logical_and(k > 0, k < nk - 1))
    def _():
        acc_ref[...] += d

    @pl.when(k == nk - 1)
    def _():
        finish(acc_ref[...] + d)


def _matmul(a, b, *, out_dtype, tm, tn, tk=None, layer=None, col0=0, n=None, b_transposed=False,
            epilogue=None, residual=None, name="matmul"):
    m, kdim = a.shape
    n_all = b.shape[-2] if b_transposed else b.shape[-1]
    n = (n_all - col0) if n is None else n
    tm, tn = min(tm, m), min(tn, n)
    tk = kdim if tk is None else tk
    nk = kdim // tk
    assert m % tm == 0 and n % tn == 0 and kdim % tk == 0
    if b_transposed:
        assert layer is not None and col0 % SUBLANES == 0
        b_spec = pl.BlockSpec((pl.Element(1), pl.Element(tn), pl.Element(tk)),
                              lambda i, j, k: (layer, pl.multiple_of(col0 + j * tn, SUBLANES), k * tk))
    else:
        assert col0 % tn == 0
        j0 = col0 // tn
        if layer is None:
            b_spec = pl.BlockSpec((tk, tn), lambda i, j, k: (k, j + j0))
        else:
            b_spec = pl.BlockSpec((None, tk, tn), lambda i, j, k: (layer, k, j + j0))
    a_buffers = 1 if (nk == 1 and tm * tk * 2 > V7X_VMEM_BYTES // 8) else 2
    in_specs = [pl.BlockSpec((tm, tk), lambda i, j, k: (i, k), pipeline_mode=pl.Buffered(a_buffers)), b_spec]
    args = [a, b]
    b_item = jnp.dtype(b.dtype).itemsize
    nbytes = a_buffers * tm * tk * 2 + 2 * tk * tn * b_item + 2 * tm * tn * jnp.dtype(out_dtype).itemsize
    if b_item != 2:
        nbytes += tk * tn * 2
    if epilogue == "residual":
        in_specs.append(pl.BlockSpec((tm, tn), lambda i, j, k: (i, j)))
        args.append(residual)
        nbytes += 2 * tm * tn * 4
    scratch = []
    if nk > 1:
        scratch.append(pltpu.VMEM((tm, tn), F32))
        nbytes += tm * tn * 4
    nbytes += tm * tn * 4
    return pl.pallas_call(
        functools.partial(_mm_kernel, nk=nk, epilogue=epilogue, b_transposed=b_transposed),
        grid=(m // tm, n // tn, nk),
        in_specs=in_specs,
        out_specs=pl.BlockSpec((tm, tn), lambda i, j, k: (i, j)),
        out_shape=jax.ShapeDtypeStruct((m, n), out_dtype),
        scratch_shapes=scratch,
        compiler_params=pltpu.CompilerParams(
            dimension_semantics=("parallel", "parallel", "arbitrary"),
            vmem_limit_bytes=_vmem_limit(nbytes)),
        name=name,
    )(*args)


def _proj_conv_kernel(h_ref, wu_ref, wcb_ref, wcc_ref, cw_ref, o_ref, wb_ref, p_ref, *, tm):
    @pl.when(pl.program_id(1) == 0)
    def _():
        wb_ref[0] = wu_ref[0].astype(BF16)
        wb_ref[1] = wcb_ref[0].astype(BF16)
        wb_ref[2] = wcc_ref[0].astype(BF16)
        p_ref[0:SUBLANES, :] = jnp.zeros((SUBLANES, p_ref.shape[1]), F32)

    h = h_ref[...]
    u = lax.dot_general(h, wb_ref[0], NT_DIMS, preferred_element_type=F32)
    cb = lax.dot_general(h, wb_ref[1], NT_DIMS, preferred_element_type=F32)
    cc = lax.dot_general(h, wb_ref[2], NT_DIMS, preferred_element_type=F32)
    p = cc * u
    p_ref[SUBLANES:SUBLANES + tm, :] = p
    p1 = p_ref[SUBLANES - 1:SUBLANES - 1 + tm, :]
    p2 = p_ref[SUBLANES - 2:SUBLANES - 2 + tm, :]
    w = cw_ref[...]
    z = w[2:3, :] * p + w[0:1, :] * p2 + w[1:2, :] * p1
    o_ref[...] = (cb * z).astype(o_ref.dtype)
    p_ref[0:SUBLANES, :] = p_ref[tm:tm + SUBLANES, :]


def _proj_gated_conv(h, w_in_t, layer, conv_w, tm=1024, tc=256):
    s, d = h.shape
    tm = min(tm, s)

    def w_spec(row0):
        return pl.BlockSpec((pl.Element(1), pl.Element(tc), pl.Element(d)),
                            lambda c, i: (layer, pl.multiple_of(row0 + c * tc, SUBLANES), 0))

    nbytes = 2 * tm * d * 2 + 3 * 2 * tc * d * 4 + 3 * tc * d * 2 + 2 * tm * tc * 2 + 6 * tm * tc * 4
    return pl.pallas_call(
        functools.partial(_proj_conv_kernel, tm=tm),
        grid=(CONV_DIM // tc, s // tm),
        in_specs=[pl.BlockSpec((tm, d), lambda c, i: (i, 0)),
                  w_spec(0), w_spec(CONV_DIM), w_spec(2 * CONV_DIM),
                  pl.BlockSpec((CONV_K, tc), lambda c, i: (0, c))],
        out_specs=pl.BlockSpec((tm, tc), lambda c, i: (i, c)),
        out_shape=jax.ShapeDtypeStruct((s, CONV_DIM), BF16),
        scratch_shapes=[pltpu.VMEM((3, tc, d), BF16), pltpu.VMEM((tm + SUBLANES, tc), F32)],
        compiler_params=pltpu.CompilerParams(
            dimension_semantics=("parallel", "arbitrary"),
            vmem_limit_bytes=_vmem_limit(nbytes)),
        name="proj_conv",
    )(h, w_in_t, w_in_t, w_in_t, conv_w)


def _t5_bucket_np(dist):
    n = np.maximum(dist, 0)
    max_exact = N_BUCKETS // 2
    nf = np.maximum(n, 1).astype(np.float64)
    large = max_exact + (np.log(nf / max_exact) / math.log(MAX_DISTANCE / max_exact)
                         * (N_BUCKETS - max_exact)).astype(np.int32)
    large = np.minimum(large, N_BUCKETS - 1)
    return np.where(n < max_exact, n, large).astype(np.int32)


def _near_bias(rel_bias):
    sl = np.arange(Q_TILE)[:, None]
    tl = np.arange(Q_TILE)[None, :]
    dist = np.stack([tl - sl + Q_TILE, tl - sl])
    assert _t5_bucket_np(np.array([Q_TILE]))[0] == N_BUCKETS - 1
    bucket = _t5_bucket_np(dist)
    rel = (rel_bias - rel_bias[N_BUCKETS - 1][None, :]) * (HEAD_DIM ** 0.5)
    onehot = np.eye(N_BUCKETS, dtype=np.float32)[bucket.reshape(-1)]
    b = jnp.dot(onehot, rel, precision=lax.Precision.HIGHEST)
    b = b.reshape(2, Q_TILE, Q_TILE, N_KV_HEADS, GROUP)
    b = b.transpose(0, 3, 1, 4, 2)
    b = b.reshape(2, N_KV_HEADS, Q_TILE, GROUP * Q_TILE).astype(F32)
    return jnp.concatenate([jnp.zeros_like(b[:1]), b], axis=0)


def _dsa_kernel(q_ref, k_ref, iq_lo_ref, iq_hi_ref, vt_ref, ik2_ref, w_ref, bias_ref, o_ref,
                sc_ref, s_ref, acc_ref, m_ref, *, topk, max_iters):
    i = pl.program_id(0)
    q0 = i * Q_TILE
    kf = float(topk)
    per = ATT_KEYS // Q_TILE
    c_last = i // per
    n_cnt = c_last + 1
    n_idx = (q0 + Q_TILE + IDX_KEYS - 1) // IDX_KEYS
    qpos = q0 + lax.broadcasted_iota(jnp.int32, (1, Q_TILE), 1)

    def fold8(x, op):
        rows = x.shape[0]
        if rows > SUBLANES * SUBLANES:
            x = op(x.reshape(SUBLANES, rows // SUBLANES, x.shape[1]), axis=0)
        return op(x.reshape(x.shape[0] // SUBLANES, SUBLANES, x.shape[1]), axis=0)

    def col_reduce(x, op):
        return op(fold8(x, op), axis=0, keepdims=True)

    n_dots = IDX_ALL // (2 * LANES)
    half = n_dots // 2

    def head_pairs(d):
        ref, e = (iq_lo_ref, d) if d < half else (iq_hi_ref, d - half)
        return jnp.concatenate([ref[:, (2 * e) * LANES:(2 * e + 1) * LANES],
                                ref[:, (2 * e + 1) * LANES:(2 * e + 2) * LANES]], axis=0)

    rhs = [head_pairs(d) for d in range(n_dots)]

    def idx_body(j, carry):
        mx, mn = carry
        s0 = pl.multiple_of(j * IDX_KEYS, IDX_KEYS)
        lhs = jnp.concatenate([ik2_ref[pl.ds(s0, IDX_KEYS), 0:LANES],
                               ik2_ref[pl.ds(s0, IDX_KEYS), LANES:2 * LANES]], axis=0)
        acc = None
        for d in range(n_dots):
            r = lax.dot_general(lhs, rhs[d], NT_DIMS, preferred_element_type=F32)
            r = jnp.maximum(r, 0.0)
            t = r[:IDX_KEYS] * w_ref[2 * d:2 * d + 1, :] + r[IDX_KEYS:] * w_ref[2 * d + 1:2 * d + 2, :]
            acc = t if acc is None else acc + t
        acc = acc[:, :Q_TILE] + acc[:, Q_TILE:]
        kpos = s0 + lax.broadcasted_iota(jnp.int32, (IDX_KEYS, 1), 0)
        causal = kpos <= qpos
        lo_fill = jnp.where(causal, acc, NEG_INF)
        sc_ref[pl.ds(s0, IDX_KEYS), :] = lo_fill
        mx = jnp.maximum(mx, fold8(lo_fill, jnp.max))
        mn = jnp.minimum(mn, fold8(jnp.where(causal, acc, -NEG_INF), jnp.min))
        return mx, mn

    mx8, mn8 = lax.fori_loop(
        0, n_idx, idx_body,
        (jnp.full((SUBLANES, Q_TILE), NEG_INF, F32), jnp.full((SUBLANES, Q_TILE), -NEG_INF, F32)))
    hi0 = jnp.max(mx8, axis=0, keepdims=True)
    lo0 = jnp.min(mn8, axis=0, keepdims=True)

    def fill_body(j, carry):
        sc_ref[pl.ds(pl.multiple_of(j * IDX_KEYS, IDX_KEYS), IDX_KEYS), :] = jnp.full(
            (IDX_KEYS, Q_TILE), NEG_INF, F32)
        return carry

    lax.fori_loop(n_idx, n_cnt * (ATT_KEYS // IDX_KEYS), fill_body, 0)

    def count_ge(tau):
        def body(c, cnt):
            blk = sc_ref[pl.ds(pl.multiple_of(c * ATT_KEYS, ATT_KEYS), ATT_KEYS), :]
            return cnt + fold8(jnp.where(blk >= tau, 1.0, 0.0), jnp.sum)
        c8 = lax.fori_loop(0, n_cnt, body, jnp.zeros((SUBLANES, Q_TILE), F32))
        return jnp.sum(c8, axis=0, keepdims=True)

    def n_active(lo, hi, flo):
        mid = 0.5 * lo + 0.5 * hi
        act = jnp.logical_and(flo > kf, jnp.logical_and(mid > lo, mid < hi))
        return act, jnp.sum(jnp.where(act, 1.0, 0.0))

    flo0 = (qpos + 1).astype(F32)

    def bis_cond(st):
        _, _, _, n, it = st
        return jnp.logical_and(n > 0.0, it < max_iters)

    def bis_body(st):
        lo, hi, flo, _, it = st
        for _ in range(BISECT_STEPS):
            act, _ = n_active(lo, hi, flo)
            mid = 0.5 * lo + 0.5 * hi
            c = count_ge(mid)
            up = jnp.logical_and(act, c >= kf)
            dn = jnp.logical_and(act, c < kf)
            lo = jnp.where(up, mid, lo)
            flo = jnp.where(up, c, flo)
            hi = jnp.where(dn, mid, hi)
        _, n = n_active(lo, hi, flo)
        return lo, hi, flo, n, it + 1

    _, n0 = n_active(lo0, hi0, flo0)
    lo, hi, flo, _, _ = lax.while_loop(bis_cond, bis_body, (lo0, hi0, flo0, n0, jnp.int32(0)))
    open_hi = jnp.logical_and(flo > kf, hi == hi0)
    n_open = jnp.sum(jnp.where(open_hi, 1.0, 0.0))
    c_hi = lax.cond(n_open > 0.0, lambda: count_ge(hi), lambda: jnp.zeros((1, Q_TILE), F32))
    tau = jnp.where(jnp.logical_and(open_hi, c_hi >= kf), hi, lo)

    c2 = (HEAD_DIM ** -0.5) * math.log2(math.e)
    qg = []
    for g in range(N_KV_HEADS):
        qg.append(jnp.concatenate(
            [q_ref[:, (g * GROUP + hh) * HEAD_DIM:(g * GROUP + hh + 1) * HEAD_DIM] for hh in range(GROUP)],
            axis=0))

    m_ref[...] = jnp.full(m_ref.shape, NEG_INF, F32)
    acc_ref[...] = jnp.zeros(acc_ref.shape, F32)

    def logits(c, g, slot):
        s0 = pl.multiple_of(jnp.minimum(c, n_cnt - 1) * ATT_KEYS, ATT_KEYS)
        kb = k_ref[pl.ds(s0, ATT_KEYS), g * HEAD_DIM:(g + 1) * HEAD_DIM]
        s_ref[slot] = lax.dot_general(kb, qg[g], NT_DIMS, preferred_element_type=F32)

    def chunk(c, near):
        s0 = pl.multiple_of(c * ATT_KEYS, ATT_KEYS)
        blk = sc_ref[pl.ds(s0, ATT_KEYS), :]
        madd = jnp.where(blk >= tau, 0.0, NEG_INF)
        madd = jnp.concatenate([madd] * GROUP, axis=1)
        for g in range(N_KV_HEADS):
            if g + 1 < N_KV_HEADS:
                logits(c, g + 1, (g + 1) % 2)
            else:
                logits(c + 1, 0, 0)
            vtb = vt_ref[g, :, pl.ds(s0, ATT_KEYS)]
            s = s_ref[g % 2] + madd
            if near:
                rows = []
                for r in range(per):
                    d = i - (c * per + r)
                    rows.append(bias_ref[jnp.where(d == 1, 1, jnp.where(d == 0, 2, 0)), g])
                s = s + jnp.concatenate(rows, axis=0)
            m_old = m_ref[g]
            m_new = jnp.maximum(m_old, col_reduce(s, jnp.max))
            m_use = jnp.where(m_new == NEG_INF, 0.0, m_new)
            alpha = jnp.exp2(c2 * (m_old - m_use))
            p = jnp.exp2(c2 * (s - m_use))
            pv = jnp.dot(vtb, p.astype(BF16), preferred_element_type=F32)
            acc_ref[g] = alpha * acc_ref[g] + pv
            m_ref[g] = m_new

    n_far = jnp.maximum(c_last - 1, 0)
    logits(0, 0, 0)

    def far_body(c, carry):
        chunk(c, False)
        return carry

    lax.fori_loop(0, n_far, far_body, 0)

    def near_body(c, carry):
        chunk(c, True)
        return carry

    lax.fori_loop(n_far, n_cnt, near_body, 0)

    for g in range(N_KV_HEADS):
        out = acc_ref[g, 0:HEAD_DIM, :] / acc_ref[g, HEAD_DIM:HEAD_DIM + 1, :]
        for hh in range(GROUP):
            h = g * GROUP + hh
            o_ref[:, h * HEAD_DIM:(h + 1) * HEAD_DIM] = (
                out[:, hh * Q_TILE:(hh + 1) * Q_TILE].T.astype(o_ref.dtype))


def _dsa_attention(zb, zc, rel_bias, topk):
    s = zb.shape[0]
    nq = s // Q_TILE
    assert s % ATT_KEYS == 0
    k_col = ATTN_DIM // KV_DIM
    iq_half = IDX_ALL // 2
    iq_col = (ATTN_DIM + 2 * KV_DIM) // iq_half
    assert (ATTN_DIM + 2 * KV_DIM) % iq_half == 0
    vt = zb[:, ATTN_DIM + KV_DIM:ATTN_DIM + 2 * KV_DIM].T.reshape(N_KV_HEADS, HEAD_DIM, s)
    vt = jnp.concatenate([vt, jnp.ones((N_KV_HEADS, V_ROWS - HEAD_DIM, s), BF16)], axis=1)
    ik = zc[:, :IDX_DIM].astype(BF16)
    ik2 = jnp.concatenate([ik, jnp.zeros((s, 2 * IDX_DIM), BF16), ik], axis=1)
    w = zc[:, IDX_DIM:IDX_DIM + IDX_HEADS] * (IDX_HEADS ** -0.5) * (IDX_DIM ** -0.5)
    w = w.reshape(nq, Q_TILE, IDX_HEADS // 4, 2, 2).transpose(0, 2, 4, 3, 1)
    w = w.reshape(nq, IDX_HEADS // 2, 2 * Q_TILE)
    bias = _near_bias(rel_bias)

    resident = dict(pipeline_mode=pl.Buffered(1))
    nbytes = (s * KV_DIM * 2 + vt.size * 2 + s * 2 * LANES * 2 + bias.size * 4 + s * Q_TILE * 4
              + 2 * (2 * Q_TILE * ATTN_DIM * 2 * 2 + IDX_HEADS * Q_TILE * 4)
              + 6 * ATT_KEYS * GROUP * Q_TILE * 4 + 3 * N_KV_HEADS * V_ROWS * GROUP * Q_TILE * 4)
    return pl.pallas_call(
        functools.partial(_dsa_kernel, topk=topk, max_iters=400),
        grid=(nq,),
        in_specs=[pl.BlockSpec((Q_TILE, ATTN_DIM), lambda i: (i, 0)),
                  pl.BlockSpec((s, KV_DIM), lambda i: (0, k_col), **resident),
                  pl.BlockSpec((Q_TILE, iq_half), lambda i: (i, iq_col)),
                  pl.BlockSpec((Q_TILE, iq_half), lambda i: (i, iq_col + 1)),
                  pl.BlockSpec(vt.shape, lambda i: (0, 0, 0), **resident),
                  pl.BlockSpec((s, 2 * LANES), lambda i: (0, 0), **resident),
                  pl.BlockSpec((None, IDX_HEADS // 2, 2 * Q_TILE), lambda i: (i, 0, 0)),
                  pl.BlockSpec(bias.shape, lambda i: (0, 0, 0, 0), **resident)],
        out_specs=pl.BlockSpec((Q_TILE, ATTN_DIM), lambda i: (i, 0)),
        out_shape=jax.ShapeDtypeStruct((s, ATTN_DIM), BF16),
        scratch_shapes=[pltpu.VMEM((s, Q_TILE), F32),
                        pltpu.VMEM((2, ATT_KEYS, GROUP * Q_TILE), F32),
                        pltpu.VMEM((N_KV_HEADS, V_ROWS, GROUP * Q_TILE), F32),
                        pltpu.VMEM((N_KV_HEADS, 1, GROUP * Q_TILE), F32)],
        compiler_params=pltpu.CompilerParams(
            dimension_semantics=("arbitrary",),
            vmem_limit_bytes=_vmem_limit(nbytes)),
        name="dsa_attention",
    )(zb, zb, zb, zb, vt, ik2, w, bias)


def _merge_kernel(a_ref, o_ref, wa_ref, wo_ref, ga_ref, gb_ref, out_ref, wab_ref, wob_ref):
    @pl.when(pl.program_id(1) == 0)
    def _():
        wab_ref[...] = wa_ref[...].astype(BF16)
        wob_ref[...] = wo_ref[...].astype(BF16)

    ya = jnp.dot(a_ref[...], wab_ref[...], preferred_element_type=F32)
    yb = jnp.dot(o_ref[...], wob_ref[...], preferred_element_type=F32)
    merged = jax.nn.sigmoid(ga_ref[...]) * ya + jax.nn.sigmoid(gb_ref[...]) * yb
    out_ref[...] = merged.astype(out_ref.dtype)


def _merge(a_in, o, w_conv_out, w_attn_out, layer, zg, g_col0, tm=512, tn=512):
    s = a_in.shape[0]
    tm = min(tm, s)
    nn = D_MODEL // tn
    assert g_col0 % tn == 0
    gj = g_col0 // tn
    nbytes = (2 * (2 * tm * CONV_DIM * 2 + 2 * CONV_DIM * tn * 4 + 2 * tm * tn * 4 + tm * tn * 2)
              + 2 * CONV_DIM * tn * 2 + 3 * tm * tn * 4)
    return pl.pallas_call(
        _merge_kernel,
        grid=(nn, s // tm),
        in_specs=[pl.BlockSpec((tm, CONV_DIM), lambda j, i: (i, 0)),
                  pl.BlockSpec((tm, ATTN_DIM), lambda j, i: (i, 0)),
                  pl.BlockSpec((None, CONV_DIM, tn), lambda j, i: (layer, 0, j)),
                  pl.BlockSpec((None, ATTN_DIM, tn), lambda j, i: (layer, 0, j)),
                  pl.BlockSpec((tm, tn), lambda j, i: (i, j + gj)),
                  pl.BlockSpec((tm, tn), lambda j, i: (i, j + gj + nn))],
        out_specs=pl.BlockSpec((tm, tn), lambda j, i: (i, j)),
        out_shape=jax.ShapeDtypeStruct((s, D_MODEL), BF16),
        scratch_shapes=[pltpu.VMEM((CONV_DIM, tn), BF16), pltpu.VMEM((ATTN_DIM, tn), BF16)],
        compiler_params=pltpu.CompilerParams(
            dimension_semantics=("parallel", "arbitrary"),
            vmem_limit_bytes=_vmem_limit(nbytes)),
        name="merge",
    )(a_in, o, w_conv_out, w_attn_out, zg, zg)


def _xattn_kernel(x_ref, gx_ref, gm_ref, wq_ref, kx_ref, vx_ref, wo_ref, out_ref, h_ref, wqb_ref, wob_ref):
    @pl.when(pl.program_id(0) == 0)
    def _():
        wqb_ref[...] = wq_ref[...].astype(BF16)
        wob_ref[...] = wo_ref[...].astype(BF16)

    def rms(v, g_ref):
        inv = lax.rsqrt(jnp.mean(v * v, axis=-1, keepdims=True) + EPS)
        return ((v * inv) * g_ref[...]).astype(BF16)

    x = x_ref[...]
    qx = jnp.dot(rms(x, gx_ref), wqb_ref[...], preferred_element_type=F32).astype(BF16)
    scale = X_HEAD_DIM ** -0.5
    outs = []
    for h in range(X_HEADS):
        sl = slice(h * X_HEAD_DIM, (h + 1) * X_HEAD_DIM)
        s = lax.dot_general(qx[:, sl], kx_ref[:, sl], NT_DIMS, preferred_element_type=F32) * scale
        m = jnp.max(s, axis=-1, keepdims=True)
        p = jnp.exp(s - m)
        l = jnp.sum(p, axis=-1, keepdims=True)
        oh = jnp.dot(p.astype(BF16), vx_ref[:, sl], preferred_element_type=F32) / l
        outs.append(oh.astype(BF16))
    o = jnp.concatenate(outs, axis=1)
    x_new = x + jnp.dot(o, wob_ref[...], preferred_element_type=F32)
    out_ref[...] = x_new
    h_ref[...] = rms(x_new, gm_ref)


def _cross_attention(x, g_xattn, g_mlp, w_xq, kv, w_xo, layer, tm=256):
    s = x.shape[0]
    tm = min(tm, s)
    n_mem = kv.shape[0]
    resident = dict(pipeline_mode=pl.Buffered(1))
    nbytes = (2 * (2 * tm * D_MODEL * 4 + tm * D_MODEL * 2) + 2 * D_MODEL * X_DIM * (4 + 2)
              + 2 * n_mem * X_DIM * 2 + 4 * tm * D_MODEL * 4)
    return pl.pallas_call(
        _xattn_kernel,
        grid=(s // tm,),
        in_specs=[pl.BlockSpec((tm, D_MODEL), lambda i: (i, 0)),
                  pl.BlockSpec((1, D_MODEL), lambda i: (0, 0)),
                  pl.BlockSpec((1, D_MODEL), lambda i: (0, 0)),
                  pl.BlockSpec((None, D_MODEL, X_DIM), lambda i: (layer, 0, 0), **resident),
                  pl.BlockSpec((n_mem, X_DIM), lambda i: (0, 0), **resident),
                  pl.BlockSpec((n_mem, X_DIM), lambda i: (0, 1), **resident),
                  pl.BlockSpec((None, X_DIM, D_MODEL), lambda i: (layer, 0, 0), **resident)],
        out_specs=[pl.BlockSpec((tm, D_MODEL), lambda i: (i, 0)),
                   pl.BlockSpec((tm, D_MODEL), lambda i: (i, 0))],
        out_shape=[jax.ShapeDtypeStruct((s, D_MODEL), F32),
                   jax.ShapeDtypeStruct((s, D_MODEL), BF16)],
        scratch_shapes=[pltpu.VMEM((D_MODEL, X_DIM), BF16), pltpu.VMEM((X_DIM, D_MODEL), BF16)],
        compiler_params=pltpu.CompilerParams(
            dimension_semantics=("arbitrary",),
            vmem_limit_bytes=_vmem_limit(nbytes)),
        name="cross_attention",
    )(x, g_xattn.reshape(1, D_MODEL), g_mlp.reshape(1, D_MODEL), w_xq, kv, kv, w_xo)


CONV_END = 3 * CONV_DIM
ATTN_END = CONV_END + ATTN_DIM + 2 * KV_DIM + IDX_ALL
IW_END = ATTN_END + IDX_DIM + IDX_HEADS


def kernel(x, mem, rel_bias, norm_mix, w_in, conv_w, w_conv_out, w_attn_out, w_mix_out, norm_xattn, norm_mem, w_xq, w_xkv, w_xo, norm_mlp, w_up, w_down, norm_final):
    bsz, s, d = x.shape
    assert bsz == 1 and d == D_MODEL
    depth = w_in.shape[0]
    topk = min(TOPK_MAX, s // 4)
    xs = x.reshape(s, d)
    mems = mem.reshape(mem.shape[1], d)
    w_in_t = jnp.swapaxes(w_in, 1, 2)
    for l in range(depth):
        h = _rmsnorm(xs, norm_mix[l], BF16)
        proj = functools.partial(_matmul, h, w_in_t, layer=l, b_transposed=True, tm=2048)
        a_in = _proj_gated_conv(h, w_in_t, l, conv_w[l])
        zb = proj(col0=CONV_END, n=ATTN_END - CONV_END, out_dtype=BF16, tn=512, name="proj_attn")
        zc = proj(col0=ATTN_END, n=LANES, out_dtype=F32, tn=LANES, name="proj_idx")
        zg = proj(col0=IW_END, n=2 * D_MODEL, out_dtype=F32, tn=512, name="proj_gate")
        o = _dsa_attention(zb, zc, rel_bias, topk)
        merged = _merge(a_in, o, w_conv_out, w_attn_out, l, zg, 0)
        xs = _matmul(merged, w_mix_out, layer=l, out_dtype=F32, tm=1024, tn=512,
                     epilogue="residual", residual=xs, name="mix_out")
        hm = _rmsnorm(mems, norm_mem[l], BF16)
        kv = _matmul(hm, w_xkv, layer=l, out_dtype=BF16, tm=256, tn=512, name="mem_kv")
        xs, hmlp = _cross_attention(xs, norm_xattn[l], norm_mlp[l], w_xq, kv, w_xo, l)
        act = _matmul(hmlp, w_up, layer=l, out_dtype=BF16, tm=2048, tn=512, epilogue="relu2", name="mlp_up")
        xs = _matmul(act, w_down, layer=l, out_dtype=F32, tm=1024, tn=512, tk=4096,
                     epilogue="residual", residual=xs, name="mlp_down")
    out = _rmsnorm(xs, norm_final, F32)
    return out.reshape(bsz, s, d)
```

```python
import functools
import math

import numpy as np
import jax
import jax.numpy as jnp
from jax import lax
from jax.experimental import pallas as pl
from jax.experimental.pallas import tpu as pltpu

F32 = jnp.float32
BF16 = jnp.bfloat16

D_MODEL = 4096
CONV_DIM = 2048
CONV_K = 3
N_HEADS = 16
N_KV_HEADS = 4
HEAD_DIM = 128
GROUP = N_HEADS // N_KV_HEADS
ATTN_DIM = N_HEADS * HEAD_DIM
KV_DIM = N_KV_HEADS * HEAD_DIM
IDX_HEADS = 32
IDX_DIM = 64
IDX_ALL = IDX_HEADS * IDX_DIM
TOPK_MAX = 256
N_BUCKETS = 32
MAX_DISTANCE = 128
X_HEADS = 4
X_HEAD_DIM = 128
X_DIM = X_HEADS * X_HEAD_DIM
EPS = 1e-6

V7X_VMEM_BYTES = 64 * 1024 * 1024
LANES = 128
SUBLANES = 8

Q_TILE = 128
IDX_KEYS = 512
ATT_KEYS = 512
V_ROWS = HEAD_DIM + 16
BISECT_STEPS = 4
NEG_INF = float("-inf")
NT_DIMS = (((1,), (1,)), ((), ()))


def _vmem_limit(nbytes):
    return int(min(nbytes + (8 << 20), V7X_VMEM_BYTES - (6 << 20)))


def _rmsnorm_kernel(x_ref, g_ref, o_ref):
    x = x_ref[...]
    inv = lax.rsqrt(jnp.mean(x * x, axis=-1, keepdims=True) + EPS)
    o_ref[...] = ((x * inv) * g_ref[...]).astype(o_ref.dtype)


def _rmsnorm(x, g, out_dtype, tr=256):
    rows, d = x.shape
    tr = min(tr, rows)
    return pl.pallas_call(
        _rmsnorm_kernel,
        grid=(rows // tr,),
        in_specs=[pl.BlockSpec((tr, d), lambda i: (i, 0)),
                  pl.BlockSpec((1, d), lambda i: (0, 0))],
        out_specs=pl.BlockSpec((tr, d), lambda i: (i, 0)),
        out_shape=jax.ShapeDtypeStruct((rows, d), out_dtype),
        compiler_params=pltpu.CompilerParams(
            dimension_semantics=("parallel",),
            vmem_limit_bytes=_vmem_limit(2 * tr * d * (4 + jnp.dtype(out_dtype).itemsize))),
        name="rmsnorm",
    )(x, g.reshape(1, d))


def _add_rmsnorm_kernel(x_ref, dx_ref, g_ref, *out_refs):
    x = x_ref[...] + dx_ref[...]
    inv = lax.rsqrt(jnp.mean(x * x, axis=-1, keepdims=True) + EPS)
    h_ref = out_refs[-1]
    h_ref[...] = ((x * inv) * g_ref[...]).astype(h_ref.dtype)
    if len(out_refs) == 2:
        out_refs[0][...] = x


def _add_rmsnorm(x, dx, g, out_dtype, keep_sum, tr=256):
    rows, d = x.shape
    tr = min(tr, rows)
    row_spec = pl.BlockSpec((tr, d), lambda i: (i, 0))
    out_specs = [row_spec, row_spec] if keep_sum else [row_spec]
    out_shape = [jax.ShapeDtypeStruct((rows, d), out_dtype)]
    if keep_sum:
        out_shape.insert(0, jax.ShapeDtypeStruct((rows, d), F32))
    outs = pl.pallas_call(
        _add_rmsnorm_kernel,
        grid=(rows // tr,),
        in_specs=[row_spec, row_spec, pl.BlockSpec((1, d), lambda i: (0, 0))],
        out_specs=out_specs,
        out_shape=out_shape,
        compiler_params=pltpu.CompilerParams(
            dimension_semantics=("parallel",),
            vmem_limit_bytes=_vmem_limit(2 * tr * d * (12 + jnp.dtype(out_dtype).itemsize))),
        name="add_rmsnorm",
    )(x, dx, g.reshape(1, d))
    return outs if keep_sum else outs[0]


def _mm_kernel(*refs, nk, epilogue, b_transposed):
    if epilogue == "residual":
        a_ref, b_ref, r_ref, o_ref = refs
    else:
        a_ref, b_ref, o_ref = refs
        r_ref = None

    def product():
        if b_transposed:
            return lax.dot_general(a_ref[...], b_ref[0].astype(BF16), NT_DIMS, preferred_element_type=F32)
        return jnp.dot(a_ref[...], b_ref[...].astype(BF16), preferred_element_type=F32)

    if nk > 1:
        @pl.when(pl.program_id(2) == 0)
        def _():
            o_ref[...] = jnp.zeros(o_ref.shape, o_ref.dtype)

        o_ref[...] += product()
        return
    acc = product()
    if epilogue == "relu2":
        acc = jnp.square(jnp.maximum(acc, 0.0))
    elif epilogue == "residual":
        acc = r_ref[...] + acc
    o_ref[...] = acc.astype(o_ref.dtype)


def _matmul(a, b, *, out_dtype, tm, tn, tk=None, layer=None, col0=0, n=None, b_transposed=False,
            epilogue=None, residual=None, name="matmul"):
    m, kdim = a.shape
    n_all = b.shape[-2] if b_transposed else b.shape[-1]
    n = (n_all - col0) if n is None else n
    tm, tn = min(tm, m), min(tn, n)
    tk = kdim if tk is None else tk
    nk = kdim // tk
    assert m % tm == 0 and n % tn == 0 and kdim % tk == 0
    if b_transposed:
        assert layer is not None and col0 % SUBLANES == 0
        b_spec = pl.BlockSpec((pl.Element(1), pl.Element(tn), pl.Element(tk)),
                              lambda i, j, k: (layer, pl.multiple_of(col0 + j * tn, SUBLANES), k * tk))
    else:
        assert col0 % tn == 0
        j0 = col0 // tn
        if layer is None:
            b_spec = pl.BlockSpec((tk, tn), lambda i, j, k: (k, j + j0))
        else:
            b_spec = pl.BlockSpec((None, tk, tn), lambda i, j, k: (layer, k, j + j0))
    a_buffers = 1 if (nk == 1 and tm * tk * 2 > V7X_VMEM_BYTES // 8) else 2
    in_specs = [pl.BlockSpec((tm, tk), lambda i, j, k: (i, k), pipeline_mode=pl.Buffered(a_buffers)), b_spec]
    args = [a, b]
    b_item = jnp.dtype(b.dtype).itemsize
    nbytes = a_buffers * tm * tk * 2 + 2 * tk * tn * b_item + 2 * tm * tn * jnp.dtype(out_dtype).itemsize
    if b_item != 2:
        nbytes += tk * tn * 2
    if epilogue == "residual":
        in_specs.append(pl.BlockSpec((tm, tn), lambda i, j, k: (i, j)))
        args.append(residual)
        nbytes += 2 * tm * tn * 4
    assert nk == 1 or (epilogue is None and out_dtype == F32)
    nbytes += tm * tn * 4
    return pl.pallas_call(
        functools.partial(_mm_kernel, nk=nk, epilogue=epilogue, b_transposed=b_transposed),
        grid=(m // tm, n // tn, nk),
        in_specs=in_specs,
        out_specs=pl.BlockSpec((tm, tn), lambda i, j, k: (i, j)),
        out_shape=jax.ShapeDtypeStruct((m, n), out_dtype),
        compiler_params=pltpu.CompilerParams(
            dimension_semantics=("parallel", "parallel", "arbitrary"),
            vmem_limit_bytes=_vmem_limit(nbytes)),
        name=name,
    )(*args)


def _proj_conv_kernel(h_ref, wu_ref, wcb_ref, wcc_ref, cw_ref, o_ref, wb_ref, p_ref, *, tm):
    @pl.when(pl.program_id(1) == 0)
    def _():
        wb_ref[0] = wu_ref[0].astype(BF16)
        wb_ref[1] = wcb_ref[0].astype(BF16)
        wb_ref[2] = wcc_ref[0].astype(BF16)
        p_ref[0:SUBLANES, :] = jnp.zeros((SUBLANES, p_ref.shape[1]), F32)

    h = h_ref[...]
    u = lax.dot_general(h, wb_ref[0], NT_DIMS, preferred_element_type=F32)
    cb = lax.dot_general(h, wb_ref[1], NT_DIMS, preferred_element_type=F32)
    cc = lax.dot_general(h, wb_ref[2], NT_DIMS, preferred_element_type=F32)
    p = cc * u
    p_ref[SUBLANES:SUBLANES + tm, :] = p
    p1 = p_ref[SUBLANES - 1:SUBLANES - 1 + tm, :]
    p2 = p_ref[SUBLANES - 2:SUBLANES - 2 + tm, :]
    w = cw_ref[...]
    z = w[2:3, :] * p + w[0:1, :] * p2 + w[1:2, :] * p1
    o_ref[...] = (cb * z).astype(o_ref.dtype)
    p_ref[0:SUBLANES, :] = p_ref[tm:tm + SUBLANES, :]


def _proj_gated_conv(h, w_in_t, layer, conv_w, tm=1024, tc=256):
    s, d = h.shape
    tm = min(tm, s)

    def w_spec(row0):
        return pl.BlockSpec((pl.Element(1), pl.Element(tc), pl.Element(d)),
                            lambda c, i: (layer, pl.multiple_of(row0 + c * tc, SUBLANES), 0))

    nbytes = 2 * tm * d * 2 + 3 * 2 * tc * d * 4 + 3 * tc * d * 2 + 2 * tm * tc * 2 + 6 * tm * tc * 4
    return pl.pallas_call(
        functools.partial(_proj_conv_kernel, tm=tm),
        grid=(CONV_DIM // tc, s // tm),
        in_specs=[pl.BlockSpec((tm, d), lambda c, i: (i, 0)),
                  w_spec(0), w_spec(CONV_DIM), w_spec(2 * CONV_DIM),
                  pl.BlockSpec((CONV_K, tc), lambda c, i: (0, c))],
        out_specs=pl.BlockSpec((tm, tc), lambda c, i: (i, c)),
        out_shape=jax.ShapeDtypeStruct((s, CONV_DIM), BF16),
        scratch_shapes=[pltpu.VMEM((3, tc, d), BF16), pltpu.VMEM((tm + SUBLANES, tc), F32)],
        compiler_params=pltpu.CompilerParams(
            dimension_semantics=("parallel", "arbitrary"),
            vmem_limit_bytes=_vmem_limit(nbytes)),
        name="proj_conv",
    )(h, w_in_t, w_in_t, w_in_t, conv_w)


def _t5_bucket_np(dist):
    n = np.maximum(dist, 0)
    max_exact = N_BUCKETS // 2
    nf = np.maximum(n, 1).astype(np.float64)
    large = max_exact + (np.log(nf / max_exact) / math.log(MAX_DISTANCE / max_exact)
                         * (N_BUCKETS - max_exact)).astype(np.int32)
    large = np.minimum(large, N_BUCKETS - 1)
    return np.where(n < max_exact, n, large).astype(np.int32)


def _near_bias(rel_bias):
    sl = np.arange(Q_TILE)[:, None]
    tl = np.arange(Q_TILE)[None, :]
    dist = np.stack([tl - sl + Q_TILE, tl - sl])
    assert _t5_bucket_np(np.array([Q_TILE]))[0] == N_BUCKETS - 1
    bucket = _t5_bucket_np(dist)
    rel = (rel_bias - rel_bias[N_BUCKETS - 1][None, :]) * (HEAD_DIM ** 0.5)
    onehot = np.eye(N_BUCKETS, dtype=np.float32)[bucket.reshape(-1)]
    b = jnp.dot(onehot, rel, precision=lax.Precision.HIGHEST)
    b = b.reshape(2, Q_TILE, Q_TILE, N_KV_HEADS, GROUP)
    b = b.transpose(0, 3, 1, 4, 2)
    b = b.reshape(2, N_KV_HEADS, Q_TILE, GROUP * Q_TILE).astype(F32)
    return jnp.concatenate([jnp.zeros_like(b[:1]), b], axis=0)


def _dsa_kernel(q_ref, k_ref, iq_lo_ref, iq_hi_ref, vt_ref, ik2_ref, w_ref, bias_ref, o_ref,
                sc_ref, s_ref, acc_ref, m_ref, *, topk, max_iters):
    i = pl.program_id(0)
    q0 = i * Q_TILE
    kf = float(topk)
    per = ATT_KEYS // Q_TILE
    c_last = i // per
    n_cnt = c_last + 1
    n_idx = (q0 + Q_TILE + IDX_KEYS - 1) // IDX_KEYS
    qpos = q0 + lax.broadcasted_iota(jnp.int32, (1, Q_TILE), 1)

    def fold8(x, op):
        rows = x.shape[0]
        if rows > SUBLANES * SUBLANES:
            x = op(x.reshape(SUBLANES, rows // SUBLANES, x.shape[1]), axis=0)
        return op(x.reshape(x.shape[0] // SUBLANES, SUBLANES, x.shape[1]), axis=0)

    def col_reduce(x, op):
        return op(fold8(x, op), axis=0, keepdims=True)

    n_dots = IDX_ALL // (2 * LANES)
    half = n_dots // 2

    def head_pairs(d):
        ref, e = (iq_lo_ref, d) if d < half else (iq_hi_ref, d - half)
        return jnp.concatenate([ref[:, (2 * e) * LANES:(2 * e + 1) * LANES],
                                ref[:, (2 * e + 1) * LANES:(2 * e + 2) * LANES]], axis=0)

    rhs = [head_pairs(d) for d in range(n_dots)]

    def idx_body(j, carry):
        mx, mn = carry
        s0 = pl.multiple_of(j * IDX_KEYS, IDX_KEYS)
        lhs = jnp.concatenate([ik2_ref[pl.ds(s0, IDX_KEYS), 0:LANES],
                               ik2_ref[pl.ds(s0, IDX_KEYS), LANES:2 * LANES]], axis=0)
        acc = None
        for d in range(n_dots):
            r = lax.dot_general(lhs, rhs[d], NT_DIMS, preferred_element_type=F32)
            r = jnp.maximum(r, 0.0)
            t = r[:IDX_KEYS] * w_ref[2 * d:2 * d + 1, :] + r[IDX_KEYS:] * w_ref[2 * d + 1:2 * d + 2, :]
            acc = t if acc is None else acc + t
        acc = acc[:, :Q_TILE] + acc[:, Q_TILE:]
        kpos = s0 + lax.broadcasted_iota(jnp.int32, (IDX_KEYS, 1), 0)
        causal = kpos <= qpos
        lo_fill = jnp.where(causal, acc, NEG_INF)
        sc_ref[pl.ds(s0, IDX_KEYS), :] = lo_fill
        mx = jnp.maximum(mx, fold8(lo_fill, jnp.max))
        mn = jnp.minimum(mn, fold8(jnp.where(causal, acc, -NEG_INF), jnp.min))
        return mx, mn

    mx8, mn8 = lax.fori_loop(
        0, n_idx, idx_body,
        (jnp.full((SUBLANES, Q_TILE), NEG_INF, F32), jnp.full((SUBLANES, Q_TILE), -NEG_INF, F32)))
    hi0 = jnp.max(mx8, axis=0, keepdims=True)
    lo0 = jnp.min(mn8, axis=0, keepdims=True)

    def fill_body(j, carry):
        sc_ref[pl.ds(pl.multiple_of(j * IDX_KEYS, IDX_KEYS), IDX_KEYS), :] = jnp.full(
            (IDX_KEYS, Q_TILE), NEG_INF, F32)
        return carry

    lax.fori_loop(n_idx, n_cnt * (ATT_KEYS // IDX_KEYS), fill_body, 0)

    def count_ge(tau):
        def body(c, cnt):
            blk = sc_ref[pl.ds(pl.multiple_of(c * ATT_KEYS, ATT_KEYS), ATT_KEYS), :]
            return cnt + fold8(jnp.where(blk >= tau, 1.0, 0.0), jnp.sum)
        c8 = lax.fori_loop(0, n_cnt, body, jnp.zeros((SUBLANES, Q_TILE), F32))
        return jnp.sum(c8, axis=0, keepdims=True)

    def n_active(lo, hi, flo):
        mid = 0.5 * lo + 0.5 * hi
        act = jnp.logical_and(flo > kf, jnp.logical_and(mid > lo, mid < hi))
        return act, jnp.sum(jnp.where(act, 1.0, 0.0))

    flo0 = (qpos + 1).astype(F32)

    def bis_cond(st):
        _, _, _, n, it = st
        return jnp.logical_and(n > 0.0, it < max_iters)

    def bis_body(st):
        lo, hi, flo, _, it = st
        for _ in range(BISECT_STEPS):
            act, _ = n_active(lo, hi, flo)
            mid = 0.5 * lo + 0.5 * hi
            c = count_ge(mid)
            up = jnp.logical_and(act, c >= kf)
            dn = jnp.logical_and(act, c < kf)
            lo = jnp.where(up, mid, lo)
            flo = jnp.where(up, c, flo)
            hi = jnp.where(dn, mid, hi)
        _, n = n_active(lo, hi, flo)
        return lo, hi, flo, n, it + 1

    _, n0 = n_active(lo0, hi0, flo0)
    lo, hi, flo, _, _ = lax.while_loop(bis_cond, bis_body, (lo0, hi0, flo0, n0, jnp.int32(0)))
    open_hi = jnp.logical_and(flo > kf, hi == hi0)
    n_open = jnp.sum(jnp.where(open_hi, 1.0, 0.0))
    c_hi = lax.cond(n_open > 0.0, lambda: count_ge(hi), lambda: jnp.zeros((1, Q_TILE), F32))
    tau = jnp.where(jnp.logical_and(open_hi, c_hi >= kf), hi, lo)

    c2 = (HEAD_DIM ** -0.5) * math.log2(math.e)
    qg = []
    for g in range(N_KV_HEADS):
        qg.append(jnp.concatenate(
            [q_ref[:, (g * GROUP + hh) * HEAD_DIM:(g * GROUP + hh + 1) * HEAD_DIM] for hh in range(GROUP)],
            axis=0))

    m_ref[...] = jnp.full(m_ref.shape, NEG_INF, F32)
    acc_ref[...] = jnp.zeros(acc_ref.shape, F32)

    def logits(c, g, slot):
        s0 = pl.multiple_of(jnp.minimum(c, n_cnt - 1) * ATT_KEYS, ATT_KEYS)
        kb = k_ref[pl.ds(s0, ATT_KEYS), g * HEAD_DIM:(g + 1) * HEAD_DIM]
        s_ref[slot] = lax.dot_general(kb, qg[g], NT_DIMS, preferred_element_type=F32)

    def chunk(c, near):
        s0 = pl.multiple_of(c * ATT_KEYS, ATT_KEYS)
        blk = sc_ref[pl.ds(s0, ATT_KEYS), :]
        madd = jnp.where(blk >= tau, 0.0, NEG_INF)
        madd = jnp.concatenate([madd] * GROUP, axis=1)
        for g in range(N_KV_HEADS):
            if g + 1 < N_KV_HEADS:
                logits(c, g + 1, (g + 1) % 2)
            else:
                logits(c + 1, 0, 0)
            vtb = vt_ref[g, :, pl.ds(s0, ATT_KEYS)]
            s = s_ref[g % 2] + madd
            if near:
                rows = []
                for r in range(per):
                    d = i - (c * per + r)
                    rows.append(bias_ref[jnp.where(d == 1, 1, jnp.where(d == 0, 2, 0)), g])
                s = s + jnp.concatenate(rows, axis=0)
            m_old = m_ref[g]
            m_new = jnp.maximum(m_old, col_reduce(s, jnp.max))
            m_use = jnp.where(m_new == NEG_INF, 0.0, m_new)
            alpha = jnp.exp2(c2 * (m_old - m_use))
            p = jnp.exp2(c2 * (s - m_use))
            pv = jnp.dot(vtb, p.astype(BF16), preferred_element_type=F32)
            acc_ref[g] = alpha * acc_ref[g] + pv
            m_ref[g] = m_new

    n_far = jnp.maximum(c_last - 1, 0)
    logits(0, 0, 0)

    def far_body(c, carry):
        chunk(c, False)
        return carry

    lax.fori_loop(0, n_far, far_body, 0)

    def near_body(c, carry):
        chunk(c, True)
        return carry

    lax.fori_loop(n_far, n_cnt, near_body, 0)

    for g in range(N_KV_HEADS):
        out = acc_ref[g, 0:HEAD_DIM, :] / acc_ref[g, HEAD_DIM:HEAD_DIM + 1, :]
        for hh in range(GROUP):
            h = g * GROUP + hh
            o_ref[:, h * HEAD_DIM:(h + 1) * HEAD_DIM] = (
                out[:, hh * Q_TILE:(hh + 1) * Q_TILE].T.astype(o_ref.dtype))


def _dsa_attention(zb, zc, rel_bias, topk):
    s = zb.shape[0]
    nq = s // Q_TILE
    assert s % ATT_KEYS == 0
    k_col = ATTN_DIM // KV_DIM
    iq_half = IDX_ALL // 2
    iq_col = (ATTN_DIM + 2 * KV_DIM) // iq_half
    assert (ATTN_DIM + 2 * KV_DIM) % iq_half == 0
    vt = zb[:, ATTN_DIM + KV_DIM:ATTN_DIM + 2 * KV_DIM].T.reshape(N_KV_HEADS, HEAD_DIM, s)
    vt = jnp.concatenate([vt, jnp.ones((N_KV_HEADS, V_ROWS - HEAD_DIM, s), BF16)], axis=1)
    ik = zc[:, :IDX_DIM].astype(BF16)
    ik2 = jnp.concatenate([ik, jnp.zeros((s, 2 * IDX_DIM), BF16), ik], axis=1)
    w = zc[:, IDX_DIM:IDX_DIM + IDX_HEADS] * (IDX_HEADS ** -0.5) * (IDX_DIM ** -0.5)
    w = w.reshape(nq, Q_TILE, IDX_HEADS // 4, 2, 2).transpose(0, 2, 4, 3, 1)
    w = w.reshape(nq, IDX_HEADS // 2, 2 * Q_TILE)
    bias = _near_bias(rel_bias)

    resident = dict(pipeline_mode=pl.Buffered(1))
    nbytes = (s * KV_DIM * 2 + vt.size * 2 + s * 2 * LANES * 2 + bias.size * 4 + s * Q_TILE * 4
              + 2 * (2 * Q_TILE * ATTN_DIM * 2 * 2 + IDX_HEADS * Q_TILE * 4)
              + 6 * ATT_KEYS * GROUP * Q_TILE * 4 + 3 * N_KV_HEADS * V_ROWS * GROUP * Q_TILE * 4)
    return pl.pallas_call(
        functools.partial(_dsa_kernel, topk=topk, max_iters=400),
        grid=(nq,),
        in_specs=[pl.BlockSpec((Q_TILE, ATTN_DIM), lambda i: (i, 0)),
                  pl.BlockSpec((s, KV_DIM), lambda i: (0, k_col), **resident),
                  pl.BlockSpec((Q_TILE, iq_half), lambda i: (i, iq_col)),
                  pl.BlockSpec((Q_TILE, iq_half), lambda i: (i, iq_col + 1)),
                  pl.BlockSpec(vt.shape, lambda i: (0, 0, 0), **resident),
                  pl.BlockSpec((s, 2 * LANES), lambda i: (0, 0), **resident),
                  pl.BlockSpec((None, IDX_HEADS // 2, 2 * Q_TILE), lambda i: (i, 0, 0)),
                  pl.BlockSpec(bias.shape, lambda i: (0, 0, 0, 0), **resident)],
        out_specs=pl.BlockSpec((Q_TILE, ATTN_DIM), lambda i: (i, 0)),
        out_shape=jax.ShapeDtypeStruct((s, ATTN_DIM), BF16),
        scratch_shapes=[pltpu.VMEM((s, Q_TILE), F32),
                        pltpu.VMEM((2, ATT_KEYS, GROUP * Q_TILE), F32),
                        pltpu.VMEM((N_KV_HEADS, V_ROWS, GROUP * Q_TILE), F32),
                        pltpu.VMEM((N_KV_HEADS, 1, GROUP * Q_TILE), F32)],
        compiler_params=pltpu.CompilerParams(
            dimension_semantics=("arbitrary",),
            vmem_limit_bytes=_vmem_limit(nbytes)),
        name="dsa_attention",
    )(zb, zb, zb, zb, vt, ik2, w, bias)


def _merge_kernel(a_ref, o_ref, wa_ref, wo_ref, ga_ref, gb_ref, out_ref, wab_ref, wob_ref):
    @pl.when(pl.program_id(1) == 0)
    def _():
        wab_ref[...] = wa_ref[...].astype(BF16)
        wob_ref[...] = wo_ref[...].astype(BF16)

    ya = jnp.dot(a_ref[...], wab_ref[...], preferred_element_type=F32)
    yb = jnp.dot(o_ref[...], wob_ref[...], preferred_element_type=F32)
    merged = jax.nn.sigmoid(ga_ref[...]) * ya + jax.nn.sigmoid(gb_ref[...]) * yb
    out_ref[...] = merged.astype(out_ref.dtype)


def _merge(a_in, o, w_conv_out, w_attn_out, layer, zg, g_col0, tm=512, tn=1024):
    s = a_in.shape[0]
    tm = min(tm, s)
    nn = D_MODEL // tn
    assert g_col0 % tn == 0
    gj = g_col0 // tn
    once = dict(pipeline_mode=pl.Buffered(1))
    nbytes = (2 * (2 * tm * CONV_DIM * 2 + 2 * tm * tn * 4 + tm * tn * 2) + 2 * CONV_DIM * tn * 4
              + 2 * CONV_DIM * tn * 2 + 4 * tm * tn * 4)
    return pl.pallas_call(
        _merge_kernel,
        grid=(nn, s // tm),
        in_specs=[pl.BlockSpec((tm, CONV_DIM), lambda j, i: (i, 0)),
                  pl.BlockSpec((tm, ATTN_DIM), lambda j, i: (i, 0)),
                  pl.BlockSpec((None, CONV_DIM, tn), lambda j, i: (layer, 0, j), **once),
                  pl.BlockSpec((None, ATTN_DIM, tn), lambda j, i: (layer, 0, j), **once),
                  pl.BlockSpec((tm, tn), lambda j, i: (i, j + gj)),
                  pl.BlockSpec((tm, tn), lambda j, i: (i, j + gj + nn))],
        out_specs=pl.BlockSpec((tm, tn), lambda j, i: (i, j)),
        out_shape=jax.ShapeDtypeStruct((s, D_MODEL), BF16),
        scratch_shapes=[pltpu.VMEM((CONV_DIM, tn), BF16), pltpu.VMEM((ATTN_DIM, tn), BF16)],
        compiler_params=pltpu.CompilerParams(
            dimension_semantics=("parallel", "arbitrary"),
            vmem_limit_bytes=_vmem_limit(nbytes)),
        name="merge",
    )(a_in, o, w_conv_out, w_attn_out, zg, zg)


def _xattn_kernel(x_ref, gx_ref, gm_ref, wq_ref, kx_ref, vx_ref, wo_ref, out_ref, h_ref, wqb_ref, wob_ref):
    @pl.when(pl.program_id(0) == 0)
    def _():
        wqb_ref[...] = wq_ref[...].astype(BF16)
        wob_ref[...] = wo_ref[...].astype(BF16)

    def rms(v, g_ref):
        inv = lax.rsqrt(jnp.mean(v * v, axis=-1, keepdims=True) + EPS)
        return ((v * inv) * g_ref[...]).astype(BF16)

    x = x_ref[...]
    qx = jnp.dot(rms(x, gx_ref), wqb_ref[...], preferred_element_type=F32).astype(BF16)
    scale = X_HEAD_DIM ** -0.5
    outs = []
    for h in range(X_HEADS):
        sl = slice(h * X_HEAD_DIM, (h + 1) * X_HEAD_DIM)
        s = lax.dot_general(qx[:, sl], kx_ref[:, sl], NT_DIMS, preferred_element_type=F32) * scale
        m = jnp.max(s, axis=-1, keepdims=True)
        p = jnp.exp(s - m)
        l = jnp.sum(p, axis=-1, keepdims=True)
        oh = jnp.dot(p.astype(BF16), vx_ref[:, sl], preferred_element_type=F32) / l
        outs.append(oh.astype(BF16))
    o = jnp.concatenate(outs, axis=1)
    x_new = x + jnp.dot(o, wob_ref[...], preferred_element_type=F32)
    out_ref[...] = x_new
    h_ref[...] = rms(x_new, gm_ref)


def _cross_attention(x, g_xattn, g_mlp, w_xq, kv, w_xo, layer, tm=256):
    s = x.shape[0]
    tm = min(tm, s)
    n_mem = kv.shape[0]
    resident = dict(pipeline_mode=pl.Buffered(1))
    nbytes = (2 * (2 * tm * D_MODEL * 4 + tm * D_MODEL * 2) + 2 * D_MODEL * X_DIM * (4 + 2)
              + 2 * n_mem * X_DIM * 2 + 4 * tm * D_MODEL * 4)
    return pl.pallas_call(
        _xattn_kernel,
        grid=(s // tm,),
        in_specs=[pl.BlockSpec((tm, D_MODEL), lambda i: (i, 0)),
                  pl.BlockSpec((1, D_MODEL), lambda i: (0, 0)),
                  pl.BlockSpec((1, D_MODEL), lambda i: (0, 0)),
                  pl.BlockSpec((None, D_MODEL, X_DIM), lambda i: (layer, 0, 0), **resident),
                  pl.BlockSpec((n_mem, X_DIM), lambda i: (0, 0), **resident),
                  pl.BlockSpec((n_mem, X_DIM), lambda i: (0, 1), **resident),
                  pl.BlockSpec((None, X_DIM, D_MODEL), lambda i: (layer, 0, 0), **resident)],
        out_specs=[pl.BlockSpec((tm, D_MODEL), lambda i: (i, 0)),
                   pl.BlockSpec((tm, D_MODEL), lambda i: (i, 0))],
        out_shape=[jax.ShapeDtypeStruct((s, D_MODEL), F32),
                   jax.ShapeDtypeStruct((s, D_MODEL), BF16)],
        scratch_shapes=[pltpu.VMEM((D_MODEL, X_DIM), BF16), pltpu.VMEM((X_DIM, D_MODEL), BF16)],
        compiler_params=pltpu.CompilerParams(
            dimension_semantics=("arbitrary",),
            vmem_limit_bytes=_vmem_limit(nbytes)),
        name="cross_attention",
    )(x, g_xattn.reshape(1, D_MODEL), g_mlp.reshape(1, D_MODEL), w_xq, kv, kv, w_xo)


CONV_END = 3 * CONV_DIM
ATTN_END = CONV_END + ATTN_DIM + 2 * KV_DIM + IDX_ALL
IW_END = ATTN_END + IDX_DIM + IDX_HEADS


def kernel(x, mem, rel_bias, norm_mix, w_in, conv_w, w_conv_out, w_attn_out, w_mix_out, norm_xattn, norm_mem, w_xq, w_xkv, w_xo, norm_mlp, w_up, w_down, norm_final):
    bsz, s, d = x.shape
    assert bsz == 1 and d == D_MODEL
    depth = w_in.shape[0]
    topk = min(TOPK_MAX, s // 4)
    xs = x.reshape(s, d)
    mems = mem.reshape(mem.shape[1], d)
    w_in_t = jnp.swapaxes(w_in, 1, 2)
    dx = None
    for l in range(depth):
        if dx is None:
            h = _rmsnorm(xs, norm_mix[l], BF16)
        else:
            xs, h = _add_rmsnorm(xs, dx, norm_mix[l], BF16, keep_sum=True)
        proj = functools.partial(_matmul, h, w_in_t, layer=l, b_transposed=True, tm=2048)
        a_in = _proj_gated_conv(h, w_in_t, l, conv_w[l])
        zb = proj(col0=CONV_END, n=ATTN_END - CONV_END, out_dtype=BF16, tn=512, name="proj_attn")
        zc = proj(col0=ATTN_END, n=LANES, out_dtype=F32, tn=LANES, name="proj_idx")
        zg = proj(col0=IW_END, n=2 * D_MODEL, out_dtype=F32, tn=512, name="proj_gate")
        o = _dsa_attention(zb, zc, rel_bias, topk)
        merged = _merge(a_in, o, w_conv_out, w_attn_out, l, zg, 0)
        xs = _matmul(merged, w_mix_out, layer=l, out_dtype=F32, tm=2048, tn=256,
                     epilogue="residual", residual=xs, name="mix_out")
        hm = _rmsnorm(mems, norm_mem[l], BF16)
        kv = _matmul(hm, w_xkv, layer=l, out_dtype=BF16, tm=256, tn=512, name="mem_kv")
        xs, hmlp = _cross_attention(xs, norm_xattn[l], norm_mlp[l], w_xq, kv, w_xo, l)
        act = _matmul(hmlp, w_up, layer=l, out_dtype=BF16, tm=2048, tn=512, epilogue="relu2", name="mlp_up")
        dx = _matmul(act, w_down, layer=l, out_dtype=F32, tm=2048, tn=1024, tk=1024, name="mlp_down")
    out = _add_rmsnorm(xs, dx, norm_final, F32, keep_sum=False)
    return out.reshape(bsz, s, d)
```

```python
import functools
import math

import numpy as np
import jax
import jax.numpy as jnp
from jax import lax
from jax.experimental import pallas as pl
from jax.experimental.pallas import tpu as pltpu

F32 = jnp.float32
BF16 = jnp.bfloat16

D_MODEL = 4096
CONV_DIM = 2048
CONV_K = 3
N_HEADS = 16
N_KV_HEADS = 4
HEAD_DIM = 128
GROUP = N_HEADS // N_KV_HEADS
ATTN_DIM = N_HEADS * HEAD_DIM
KV_DIM = N_KV_HEADS * HEAD_DIM
IDX_HEADS = 32
IDX_DIM = 64
IDX_ALL = IDX_HEADS * IDX_DIM
TOPK_MAX = 256
N_BUCKETS = 32
MAX_DISTANCE = 128
X_HEADS = 4
X_HEAD_DIM = 128
X_DIM = X_HEADS * X_HEAD_DIM
EPS = 1e-6

V7X_VMEM_BYTES = 64 * 1024 * 1024
LANES = 128
SUBLANES = 8

Q_TILE = 128
IDX_KEYS = 512
ATT_KEYS = 512
V_ROWS = HEAD_DIM + 16
BISECT_STEPS = 4
NEG_INF = float("-inf")
NT_DIMS = (((1,), (1,)), ((), ()))


def _vmem_limit(nbytes):
    return int(min(nbytes + (8 << 20), V7X_VMEM_BYTES - (6 << 20)))


def _rmsnorm_kernel(x_ref, g_ref, o_ref):
    x = x_ref[...]
    inv = lax.rsqrt(jnp.mean(x * x, axis=-1, keepdims=True) + EPS)
    o_ref[...] = ((x * inv) * g_ref[...]).astype(o_ref.dtype)


def _rmsnorm(x, g, out_dtype, tr=256):
    rows, d = x.shape
    tr = min(tr, rows)
    return pl.pallas_call(
        _rmsnorm_kernel,
        grid=(rows // tr,),
        in_specs=[pl.BlockSpec((tr, d), lambda i: (i, 0)),
                  pl.BlockSpec((1, d), lambda i: (0, 0))],
        out_specs=pl.BlockSpec((tr, d), lambda i: (i, 0)),
        out_shape=jax.ShapeDtypeStruct((rows, d), out_dtype),
        compiler_params=pltpu.CompilerParams(
            dimension_semantics=("parallel",),
            vmem_limit_bytes=_vmem_limit(2 * tr * d * (4 + jnp.dtype(out_dtype).itemsize))),
        name="rmsnorm",
    )(x, g.reshape(1, d))


def _add_rmsnorm_kernel(x_ref, dx_ref, g_ref, *out_refs):
    x = x_ref[...] + dx_ref[...]
    inv = lax.rsqrt(jnp.mean(x * x, axis=-1, keepdims=True) + EPS)
    h_ref = out_refs[-1]
    h_ref[...] = ((x * inv) * g_ref[...]).astype(h_ref.dtype)
    if len(out_refs) == 2:
        out_refs[0][...] = x


def _add_rmsnorm(x, dx, g, out_dtype, keep_sum, tr=256):
    rows, d = x.shape
    tr = min(tr, rows)
    row_spec = pl.BlockSpec((tr, d), lambda i: (i, 0))
    out_specs = [row_spec, row_spec] if keep_sum else [row_spec]
    out_shape = [jax.ShapeDtypeStruct((rows, d), out_dtype)]
    if keep_sum:
        out_shape.insert(0, jax.ShapeDtypeStruct((rows, d), F32))
    outs = pl.pallas_call(
        _add_rmsnorm_kernel,
        grid=(rows // tr,),
        in_specs=[row_spec, row_spec, pl.BlockSpec((1, d), lambda i: (0, 0))],
        out_specs=out_specs,
        out_shape=out_shape,
        compiler_params=pltpu.CompilerParams(
            dimension_semantics=("parallel",),
            vmem_limit_bytes=_vmem_limit(2 * tr * d * (12 + jnp.dtype(out_dtype).itemsize))),
        name="add_rmsnorm",
    )(x, dx, g.reshape(1, d))
    return outs if keep_sum else outs[0]


def _mm_kernel(*refs, nk, epilogue, b_transposed):
    if epilogue == "residual":
        a_ref, b_ref, r_ref, o_ref = refs
    else:
        a_ref, b_ref, o_ref = refs
        r_ref = None

    def product():
        if b_transposed:
            return lax.dot_general(a_ref[...], b_ref[0].astype(BF16), NT_DIMS, preferred_element_type=F32)
        return jnp.dot(a_ref[...], b_ref[...].astype(BF16), preferred_element_type=F32)

    if nk > 1:
        @pl.when(pl.program_id(2) == 0)
        def _():
            o_ref[...] = jnp.zeros(o_ref.shape, o_ref.dtype)

        o_ref[...] += product()
        return
    acc = product()
    if epilogue == "relu2":
        acc = jnp.square(jnp.maximum(acc, 0.0))
    elif epilogue == "residual":
        acc = r_ref[...] + acc
    o_ref[...] = acc.astype(o_ref.dtype)


def _matmul(a, b, *, out_dtype, tm, tn, tk=None, layer=None, col0=0, n=None, b_transposed=False,
            epilogue=None, residual=None, name="matmul"):
    m, kdim = a.shape
    n_all = b.shape[-2] if b_transposed else b.shape[-1]
    n = (n_all - col0) if n is None else n
    tm, tn = min(tm, m), min(tn, n)
    tk = kdim if tk is None else tk
    nk = kdim // tk
    assert m % tm == 0 and n % tn == 0 and kdim % tk == 0
    if b_transposed:
        assert layer is not None and col0 % SUBLANES == 0
        b_spec = pl.BlockSpec((pl.Element(1), pl.Element(tn), pl.Element(tk)),
                              lambda i, j, k: (layer, pl.multiple_of(col0 + j * tn, SUBLANES), k * tk))
    else:
        assert col0 % tn == 0
        j0 = col0 // tn
        if layer is None:
            b_spec = pl.BlockSpec((tk, tn), lambda i, j, k: (k, j + j0))
        else:
            b_spec = pl.BlockSpec((None, tk, tn), lambda i, j, k: (layer, k, j + j0))
    a_buffers = 1 if (nk == 1 and tm * tk * 2 > V7X_VMEM_BYTES // 8) else 2
    in_specs = [pl.BlockSpec((tm, tk), lambda i, j, k: (i, k), pipeline_mode=pl.Buffered(a_buffers)), b_spec]
    args = [a, b]
    b_item = jnp.dtype(b.dtype).itemsize
    nbytes = a_buffers * tm * tk * 2 + 2 * tk * tn * b_item + 2 * tm * tn * jnp.dtype(out_dtype).itemsize
    if b_item != 2:
        nbytes += tk * tn * 2
    if epilogue == "residual":
        in_specs.append(pl.BlockSpec((tm, tn), lambda i, j, k: (i, j)))
        args.append(residual)
        nbytes += 2 * tm * tn * 4
    assert nk == 1 or (epilogue is None and out_dtype == F32)
    nbytes += tm * tn * 4
    return pl.pallas_call(
        functools.partial(_mm_kernel, nk=nk, epilogue=epilogue, b_transposed=b_transposed),
        grid=(m // tm, n // tn, nk),
        in_specs=in_specs,
        out_specs=pl.BlockSpec((tm, tn), lambda i, j, k: (i, j)),
        out_shape=jax.ShapeDtypeStruct((m, n), out_dtype),
        compiler_params=pltpu.CompilerParams(
            dimension_semantics=("parallel", "parallel", "arbitrary"),
            vmem_limit_bytes=_vmem_limit(nbytes)),
        name=name,
    )(*args)


def _proj_conv_kernel(h_ref, wu_ref, wcb_ref, wcc_ref, cw_ref, o_ref, wb_ref, p_ref, *, tm):
    @pl.when(pl.program_id(1) == 0)
    def _():
        wb_ref[0] = wu_ref[0].astype(BF16)
        wb_ref[1] = wcb_ref[0].astype(BF16)
        wb_ref[2] = wcc_ref[0].astype(BF16)
        p_ref[0:SUBLANES, :] = jnp.zeros((SUBLANES, p_ref.shape[1]), F32)

    h = h_ref[...]
    u = lax.dot_general(h, wb_ref[0], NT_DIMS, preferred_element_type=F32)
    cb = lax.dot_general(h, wb_ref[1], NT_DIMS, preferred_element_type=F32)
    cc = lax.dot_general(h, wb_ref[2], NT_DIMS, preferred_element_type=F32)
    p = cc * u
    p_ref[SUBLANES:SUBLANES + tm, :] = p
    p1 = p_ref[SUBLANES - 1:SUBLANES - 1 + tm, :]
    p2 = p_ref[SUBLANES - 2:SUBLANES - 2 + tm, :]
    w = cw_ref[...]
    z = w[2:3, :] * p + w[0:1, :] * p2 + w[1:2, :] * p1
    o_ref[...] = (cb * z).astype(o_ref.dtype)
    p_ref[0:SUBLANES, :] = p_ref[tm:tm + SUBLANES, :]


def _proj_gated_conv(h, w_in_t, layer, conv_w, tm=1024, tc=256):
    s, d = h.shape
    tm = min(tm, s)

    def w_spec(row0):
        return pl.BlockSpec((pl.Element(1), pl.Element(tc), pl.Element(d)),
                            lambda c, i: (layer, pl.multiple_of(row0 + c * tc, SUBLANES), 0))

    nbytes = 2 * tm * d * 2 + 3 * 2 * tc * d * 4 + 3 * tc * d * 2 + 2 * tm * tc * 2 + 6 * tm * tc * 4
    return pl.pallas_call(
        functools.partial(_proj_conv_kernel, tm=tm),
        grid=(CONV_DIM // tc, s // tm),
        in_specs=[pl.BlockSpec((tm, d), lambda c, i: (i, 0)),
                  w_spec(0), w_spec(CONV_DIM), w_spec(2 * CONV_DIM),
                  pl.BlockSpec((CONV_K, tc), lambda c, i: (0, c))],
        out_specs=pl.BlockSpec((tm, tc), lambda c, i: (i, c)),
        out_shape=jax.ShapeDtypeStruct((s, CONV_DIM), BF16),
        scratch_shapes=[pltpu.VMEM((3, tc, d), BF16), pltpu.VMEM((tm + SUBLANES, tc), F32)],
        compiler_params=pltpu.CompilerParams(
            dimension_semantics=("parallel", "arbitrary"),
            vmem_limit_bytes=_vmem_limit(nbytes)),
        name="proj_conv",
    )(h, w_in_t, w_in_t, w_in_t, conv_w)


def _t5_bucket_np(dist):
    n = np.maximum(dist, 0)
    max_exact = N_BUCKETS // 2
    nf = np.maximum(n, 1).astype(np.float64)
    large = max_exact + (np.log(nf / max_exact) / math.log(MAX_DISTANCE / max_exact)
                         * (N_BUCKETS - max_exact)).astype(np.int32)
    large = np.minimum(large, N_BUCKETS - 1)
    return np.where(n < max_exact, n, large).astype(np.int32)


def _near_bias(rel_bias):
    sl = np.arange(Q_TILE)[:, None]
    tl = np.arange(Q_TILE)[None, :]
    dist = np.stack([tl - sl + Q_TILE, tl - sl])
    assert _t5_bucket_np(np.array([Q_TILE]))[0] == N_BUCKETS - 1
    bucket = _t5_bucket_np(dist)
    rel = (rel_bias - rel_bias[N_BUCKETS - 1][None, :]) * (HEAD_DIM ** 0.5)
    onehot = np.eye(N_BUCKETS, dtype=np.float32)[bucket.reshape(-1)]
    b = jnp.dot(onehot, rel, precision=lax.Precision.HIGHEST)
    b = b.reshape(2, Q_TILE, Q_TILE, N_KV_HEADS, GROUP)
    b = b.transpose(0, 3, 1, 4, 2)
    b = b.reshape(2, N_KV_HEADS, Q_TILE, GROUP * Q_TILE).astype(F32)
    return jnp.concatenate([jnp.zeros_like(b[:1]), b], axis=0)


def _dsa_kernel(q_ref, k_ref, iq_lo_ref, iq_hi_ref, vt_ref, ik2_ref, w_ref, bias_ref, o_ref,
                sc_ref, s_ref, acc_ref, m_ref, *, topk, max_iters, pos_bits):
    i = pl.program_id(0)
    q0 = i * Q_TILE
    kf = float(topk)
    per = ATT_KEYS // Q_TILE
    c_last = i // per
    n_cnt = c_last + 1
    n_idx = (q0 + Q_TILE + IDX_KEYS - 1) // IDX_KEYS
    qpos = q0 + lax.broadcasted_iota(jnp.int32, (1, Q_TILE), 1)

    def fold8(x, op):
        rows = x.shape[0]
        if rows > SUBLANES * SUBLANES:
            x = op(x.reshape(SUBLANES, rows // SUBLANES, x.shape[1]), axis=0)
        return op(x.reshape(x.shape[0] // SUBLANES, SUBLANES, x.shape[1]), axis=0)

    def col_reduce(x, op):
        return op(fold8(x, op), axis=0, keepdims=True)

    n_dots = IDX_ALL // (2 * LANES)
    half = n_dots // 2

    def head_pairs(d):
        ref, e = (iq_lo_ref, d) if d < half else (iq_hi_ref, d - half)
        return jnp.concatenate([ref[:, (2 * e) * LANES:(2 * e + 1) * LANES],
                                ref[:, (2 * e + 1) * LANES:(2 * e + 2) * LANES]], axis=0)

    rhs = [head_pairs(d) for d in range(n_dots)]

    def idx_body(j, carry):
        mx, mn = carry
        s0 = pl.multiple_of(j * IDX_KEYS, IDX_KEYS)
        lhs = jnp.concatenate([ik2_ref[pl.ds(s0, IDX_KEYS), 0:LANES],
                               ik2_ref[pl.ds(s0, IDX_KEYS), LANES:2 * LANES]], axis=0)
        acc = None
        for d in range(n_dots):
            r = lax.dot_general(lhs, rhs[d], NT_DIMS, preferred_element_type=F32)
            r = jnp.maximum(r, 0.0)
            t = r[:IDX_KEYS] * w_ref[2 * d:2 * d + 1, :] + r[IDX_KEYS:] * w_ref[2 * d + 1:2 * d + 2, :]
            acc = t if acc is None else acc + t
        acc = acc[:, :Q_TILE] + acc[:, Q_TILE:]
        kpos = s0 + lax.broadcasted_iota(jnp.int32, (IDX_KEYS, 1), 0)
        causal = kpos <= qpos
        lo_fill = jnp.where(causal, acc, NEG_INF)
        sc_ref[pl.ds(s0, IDX_KEYS), :] = lo_fill
        mx = jnp.maximum(mx, fold8(lo_fill, jnp.max))
        mn = jnp.minimum(mn, fold8(jnp.where(causal, acc, -NEG_INF), jnp.min))
        return mx, mn

    mx8, mn8 = lax.fori_loop(
        0, n_idx, idx_body,
        (jnp.full((SUBLANES, Q_TILE), NEG_INF, F32), jnp.full((SUBLANES, Q_TILE), -NEG_INF, F32)))
    hi0 = jnp.max(mx8, axis=0, keepdims=True)
    lo0 = jnp.min(mn8, axis=0, keepdims=True)

    def fill_body(j, carry):
        sc_ref[pl.ds(pl.multiple_of(j * IDX_KEYS, IDX_KEYS), IDX_KEYS), :] = jnp.full(
            (IDX_KEYS, Q_TILE), NEG_INF, F32)
        return carry

    lax.fori_loop(n_idx, n_cnt * (ATT_KEYS // IDX_KEYS), fill_body, 0)

    def count_ge(tau):
        def body(c, cnt):
            blk = sc_ref[pl.ds(pl.multiple_of(c * ATT_KEYS, ATT_KEYS), ATT_KEYS), :]
            return cnt + fold8(jnp.where(blk >= tau, 1.0, 0.0), jnp.sum)
        c8 = lax.fori_loop(0, n_cnt, body, jnp.zeros((SUBLANES, Q_TILE), F32))
        return jnp.sum(c8, axis=0, keepdims=True)

    def n_active(lo, hi, flo):
        mid = 0.5 * lo + 0.5 * hi
        act = jnp.logical_and(flo > kf, jnp.logical_and(mid > lo, mid < hi))
        return act, jnp.sum(jnp.where(act, 1.0, 0.0))

    flo0 = (qpos + 1).astype(F32)

    def bis_cond(st):
        _, _, _, n, it = st
        return jnp.logical_and(n > 0.0, it < max_iters)

    def bis_body(st):
        lo, hi, flo, _, it = st
        for _ in range(BISECT_STEPS):
            act, _ = n_active(lo, hi, flo)
            mid = 0.5 * lo + 0.5 * hi
            c = count_ge(mid)
            up = jnp.logical_and(act, c >= kf)
            dn = jnp.logical_and(act, c < kf)
            lo = jnp.where(up, mid, lo)
            flo = jnp.where(up, c, flo)
            hi = jnp.where(dn, mid, hi)
        _, n = n_active(lo, hi, flo)
        return lo, hi, flo, n, it + 1

    _, n0 = n_active(lo0, hi0, flo0)
    lo, hi, flo, _, _ = lax.while_loop(bis_cond, bis_body, (lo0, hi0, flo0, n0, jnp.int32(0)))
    open_hi = jnp.logical_and(flo > kf, hi == hi0)
    n_open = jnp.sum(jnp.where(open_hi, 1.0, 0.0))
    c_hi = lax.cond(n_open > 0.0, lambda: count_ge(hi), lambda: jnp.zeros((1, Q_TILE), F32))
    use_hi = jnp.logical_and(open_hi, c_hi >= kf)
    tau = jnp.where(use_hi, hi, lo)
    n_ge = jnp.where(use_hi, c_hi, flo)

    def key_pos(c):
        return c * ATT_KEYS + lax.broadcasted_iota(jnp.int32, (ATT_KEYS, 1), 0)

    def count_rows(pred):
        def body(c, cnt):
            blk = sc_ref[pl.ds(pl.multiple_of(c * ATT_KEYS, ATT_KEYS), ATT_KEYS), :]
            return cnt + fold8(pred(blk, key_pos(c)), jnp.sum)
        c8 = lax.fori_loop(0, n_cnt, body, jnp.zeros((SUBLANES, Q_TILE), F32))
        return jnp.sum(c8, axis=0, keepdims=True)

    tied = n_ge > kf
    n_keys = n_cnt * ATT_KEYS

    def tie_cut():
        n_gt = count_rows(lambda blk, pos: jnp.where(blk > tau, 1.0, 0.0))
        need = kf - n_gt

        def step(_, st):
            lo_p, hi_p = st
            mid = lax.shift_right_logical(lo_p + hi_p, 1)
            below = count_rows(
                lambda blk, pos: jnp.where(blk == tau, jnp.where(pos < mid, 1.0, 0.0), 0.0))
            ge = below >= need
            return jnp.where(ge, lo_p, mid), jnp.where(ge, mid, hi_p)

        lo_p = jnp.zeros((1, Q_TILE), jnp.int32)
        hi_p = jnp.zeros((1, Q_TILE), jnp.int32) + n_keys
        _, hi_p = lax.fori_loop(0, pos_bits, step, (lo_p, hi_p))
        return hi_p

    n_tied = jnp.sum(jnp.where(tied, 1.0, 0.0))
    cut = lax.cond(n_tied > 0.0, tie_cut, lambda: jnp.zeros((1, Q_TILE), jnp.int32) + n_keys)
    cut = jnp.where(tied, cut, n_keys)

    def mask_body(c, carry):
        rows = pl.ds(pl.multiple_of(c * ATT_KEYS, ATT_KEYS), ATT_KEYS)
        blk = sc_ref[rows, :]
        at_tau = jnp.where(key_pos(c) < cut, 0.0, NEG_INF)
        sc_ref[rows, :] = jnp.where(blk > tau, 0.0, jnp.where(blk == tau, at_tau, NEG_INF))
        return carry

    lax.fori_loop(0, n_cnt, mask_body, 0)

    c2 = (HEAD_DIM ** -0.5) * math.log2(math.e)
    qg = []
    for g in range(N_KV_HEADS):
        qg.append(jnp.concatenate(
            [q_ref[:, (g * GROUP + hh) * HEAD_DIM:(g * GROUP + hh + 1) * HEAD_DIM] for hh in range(GROUP)],
            axis=0))

    m_ref[...] = jnp.full(m_ref.shape, NEG_INF, F32)
    acc_ref[...] = jnp.zeros(acc_ref.shape, F32)

    def logits(c, g, slot):
        s0 = pl.multiple_of(jnp.minimum(c, n_cnt - 1) * ATT_KEYS, ATT_KEYS)
        kb = k_ref[pl.ds(s0, ATT_KEYS), g * HEAD_DIM:(g + 1) * HEAD_DIM]
        s_ref[slot] = lax.dot_general(kb, qg[g], NT_DIMS, preferred_element_type=F32)

    def chunk(c, near):
        s0 = pl.multiple_of(c * ATT_KEYS, ATT_KEYS)
        madd = jnp.concatenate([sc_ref[pl.ds(s0, ATT_KEYS), :]] * GROUP, axis=1)
        for g in range(N_KV_HEADS):
            if g + 1 < N_KV_HEADS:
                logits(c, g + 1, (g + 1) % 2)
            else:
                logits(c + 1, 0, 0)
            vtb = vt_ref[g, :, pl.ds(s0, ATT_KEYS)]
            s = s_ref[g % 2] + madd
            if near:
                rows = []
                for r in range(per):
                    d = i - (c * per + r)
                    rows.append(bias_ref[jnp.where(d == 1, 1, jnp.where(d == 0, 2, 0)), g])
                s = s + jnp.concatenate(rows, axis=0)
            m_old = m_ref[g]
            m_new = jnp.maximum(m_old, col_reduce(s, jnp.max))
            m_use = jnp.where(m_new == NEG_INF, 0.0, m_new)
            alpha = jnp.exp2(c2 * (m_old - m_use))
            p = jnp.exp2(c2 * (s - m_use))
            pv = jnp.dot(vtb, p.astype(BF16), preferred_element_type=F32)
            acc_ref[g] = alpha * acc_ref[g] + pv
            m_ref[g] = m_new

    n_far = jnp.maximum(c_last - 1, 0)
    logits(0, 0, 0)

    def far_body(c, carry):
        chunk(c, False)
        return carry

    lax.fori_loop(0, n_far, far_body, 0)

    def near_body(c, carry):
        chunk(c, True)
        return carry

    lax.fori_loop(n_far, n_cnt, near_body, 0)

    for g in range(N_KV_HEADS):
        out = acc_ref[g, 0:HEAD_DIM, :] / acc_ref[g, HEAD_DIM:HEAD_DIM + 1, :]
        for hh in range(GROUP):
            h = g * GROUP + hh
            o_ref[:, h * HEAD_DIM:(h + 1) * HEAD_DIM] = (
                out[:, hh * Q_TILE:(hh + 1) * Q_TILE].T.astype(o_ref.dtype))


def _dsa_attention(zb, zc, rel_bias, topk):
    s = zb.shape[0]
    nq = s // Q_TILE
    assert s % ATT_KEYS == 0
    k_col = ATTN_DIM // KV_DIM
    iq_half = IDX_ALL // 2
    iq_col = (ATTN_DIM + 2 * KV_DIM) // iq_half
    assert (ATTN_DIM + 2 * KV_DIM) % iq_half == 0
    vt = zb[:, ATTN_DIM + KV_DIM:ATTN_DIM + 2 * KV_DIM].T.reshape(N_KV_HEADS, HEAD_DIM, s)
    vt = jnp.concatenate([vt, jnp.ones((N_KV_HEADS, V_ROWS - HEAD_DIM, s), BF16)], axis=1)
    ik = zc[:, :IDX_DIM].astype(BF16)
    ik2 = jnp.concatenate([ik, jnp.zeros((s, 2 * IDX_DIM), BF16), ik], axis=1)
    w = zc[:, IDX_DIM:IDX_DIM + IDX_HEADS] * (IDX_HEADS ** -0.5) * (IDX_DIM ** -0.5)
    w = w.reshape(nq, Q_TILE, IDX_HEADS // 4, 2, 2).transpose(0, 2, 4, 3, 1)
    w = w.reshape(nq, IDX_HEADS // 2, 2 * Q_TILE)
    bias = _near_bias(rel_bias)

    resident = dict(pipeline_mode=pl.Buffered(1))
    nbytes = (s * KV_DIM * 2 + vt.size * 2 + s * 2 * LANES * 2 + bias.size * 4 + s * Q_TILE * 4
              + 2 * (2 * Q_TILE * ATTN_DIM * 2 * 2 + IDX_HEADS * Q_TILE * 4)
              + 6 * ATT_KEYS * GROUP * Q_TILE * 4 + 3 * N_KV_HEADS * V_ROWS * GROUP * Q_TILE * 4)
    return pl.pallas_call(
        functools.partial(_dsa_kernel, topk=topk, max_iters=400, pos_bits=int(s).bit_length()),
        grid=(nq,),
        in_specs=[pl.BlockSpec((Q_TILE, ATTN_DIM), lambda i: (i, 0)),
                  pl.BlockSpec((s, KV_DIM), lambda i: (0, k_col), **resident),
                  pl.BlockSpec((Q_TILE, iq_half), lambda i: (i, iq_col)),
                  pl.BlockSpec((Q_TILE, iq_half), lambda i: (i, iq_col + 1)),
                  pl.BlockSpec(vt.shape, lambda i: (0, 0, 0), **resident),
                  pl.BlockSpec((s, 2 * LANES), lambda i: (0, 0), **resident),
                  pl.BlockSpec((None, IDX_HEADS // 2, 2 * Q_TILE), lambda i: (i, 0, 0)),
                  pl.BlockSpec(bias.shape, lambda i: (0, 0, 0, 0), **resident)],
        out_specs=pl.BlockSpec((Q_TILE, ATTN_DIM), lambda i: (i, 0)),
        out_shape=jax.ShapeDtypeStruct((s, ATTN_DIM), BF16),
        scratch_shapes=[pltpu.VMEM((s, Q_TILE), F32),
                        pltpu.VMEM((2, ATT_KEYS, GROUP * Q_TILE), F32),
                        pltpu.VMEM((N_KV_HEADS, V_ROWS, GROUP * Q_TILE), F32),
                        pltpu.VMEM((N_KV_HEADS, 1, GROUP * Q_TILE), F32)],
        compiler_params=pltpu.CompilerParams(
            dimension_semantics=("arbitrary",),
            vmem_limit_bytes=_vmem_limit(nbytes)),
        name="dsa_attention",
    )(zb, zb, zb, zb, vt, ik2, w, bias)


def _merge_kernel(a_ref, o_ref, wa_ref, wo_ref, ga_ref, gb_ref, out_ref, wab_ref, wob_ref):
    @pl.when(pl.program_id(1) == 0)
    def _():
        wab_ref[...] = wa_ref[...].astype(BF16)
        wob_ref[...] = wo_ref[...].astype(BF16)

    ya = jnp.dot(a_ref[...], wab_ref[...], preferred_element_type=F32)
    yb = jnp.dot(o_ref[...], wob_ref[...], preferred_element_type=F32)
    merged = jax.nn.sigmoid(ga_ref[...]) * ya + jax.nn.sigmoid(gb_ref[...]) * yb
    out_ref[...] = merged.astype(out_ref.dtype)


def _merge(a_in, o, w_conv_out, w_attn_out, layer, zg, g_col0, tm=512, tn=1024):
    s = a_in.shape[0]
    tm = min(tm, s)
    nn = D_MODEL // tn
    assert g_col0 % tn == 0
    gj = g_col0 // tn
    once = dict(pipeline_mode=pl.Buffered(1))
    nbytes = (2 * (2 * tm * CONV_DIM * 2 + 2 * tm * tn * 4 + tm * tn * 2) + 2 * CONV_DIM * tn * 4
              + 2 * CONV_DIM * tn * 2 + 4 * tm * tn * 4)
    return pl.pallas_call(
        _merge_kernel,
        grid=(nn, s // tm),
        in_specs=[pl.BlockSpec((tm, CONV_DIM), lambda j, i: (i, 0)),
                  pl.BlockSpec((tm, ATTN_DIM), lambda j, i: (i, 0)),
                  pl.BlockSpec((None, CONV_DIM, tn), lambda j, i: (layer, 0, j), **once),
                  pl.BlockSpec((None, ATTN_DIM, tn), lambda j, i: (layer, 0, j), **once),
                  pl.BlockSpec((tm, tn), lambda j, i: (i, j + gj)),
                  pl.BlockSpec((tm, tn), lambda j, i: (i, j + gj + nn))],
        out_specs=pl.BlockSpec((tm, tn), lambda j, i: (i, j)),
        out_shape=jax.ShapeDtypeStruct((s, D_MODEL), BF16),
        scratch_shapes=[pltpu.VMEM((CONV_DIM, tn), BF16), pltpu.VMEM((ATTN_DIM, tn), BF16)],
        compiler_params=pltpu.CompilerParams(
            dimension_semantics=("parallel", "arbitrary"),
            vmem_limit_bytes=_vmem_limit(nbytes)),
        name="merge",
    )(a_in, o, w_conv_out, w_attn_out, zg, zg)


def _xattn_kernel(x_ref, gx_ref, gm_ref, wq_ref, kx_ref, vx_ref, wo_ref, out_ref, h_ref, wqb_ref, wob_ref):
    @pl.when(pl.program_id(0) == 0)
    def _():
        wqb_ref[...] = wq_ref[...].astype(BF16)
        wob_ref[...] = wo_ref[...].astype(BF16)

    def rms(v, g_ref):
        inv = lax.rsqrt(jnp.mean(v * v, axis=-1, keepdims=True) + EPS)
        return ((v * inv) * g_ref[...]).astype(BF16)

    x = x_ref[...]
    qx = jnp.dot(rms(x, gx_ref), wqb_ref[...], preferred_element_type=F32).astype(BF16)
    scale = X_HEAD_DIM ** -0.5
    outs = []
    for h in range(X_HEADS):
        sl = slice(h * X_HEAD_DIM, (h + 1) * X_HEAD_DIM)
        s = lax.dot_general(qx[:, sl], kx_ref[:, sl], NT_DIMS, preferred_element_type=F32) * scale
        m = jnp.max(s, axis=-1, keepdims=True)
        p = jnp.exp(s - m)
        l = jnp.sum(p, axis=-1, keepdims=True)
        oh = jnp.dot(p.astype(BF16), vx_ref[:, sl], preferred_element_type=F32) / l
        outs.append(oh.astype(BF16))
    o = jnp.concatenate(outs, axis=1)
    x_new = x + jnp.dot(o, wob_ref[...], preferred_element_type=F32)
    out_ref[...] = x_new
    h_ref[...] = rms(x_new, gm_ref)


def _cross_attention(x, g_xattn, g_mlp, w_xq, kv, w_xo, layer, tm=256):
    s = x.shape[0]
    tm = min(tm, s)
    n_mem = kv.shape[0]
    resident = dict(pipeline_mode=pl.Buffered(1))
    nbytes = (2 * (2 * tm * D_MODEL * 4 + tm * D_MODEL * 2) + 2 * D_MODEL * X_DIM * (4 + 2)
              + 2 * n_mem * X_DIM * 2 + 4 * tm * D_MODEL * 4)
    return pl.pallas_call(
        _xattn_kernel,
        grid=(s // tm,),
        in_specs=[pl.BlockSpec((tm, D_MODEL), lambda i: (i, 0)),
                  pl.BlockSpec((1, D_MODEL), lambda i: (0, 0)),
                  pl.BlockSpec((1, D_MODEL), lambda i: (0, 0)),
                  pl.BlockSpec((None, D_MODEL, X_DIM), lambda i: (layer, 0, 0), **resident),
                  pl.BlockSpec((n_mem, X_DIM), lambda i: (0, 0), **resident),
                  pl.BlockSpec((n_mem, X_DIM), lambda i: (0, 1), **resident),
                  pl.BlockSpec((None, X_DIM, D_MODEL), lambda i: (layer, 0, 0), **resident)],
        out_specs=[pl.BlockSpec((tm, D_MODEL), lambda i: (i, 0)),
                   pl.BlockSpec((tm, D_MODEL), lambda i: (i, 0))],
        out_shape=[jax.ShapeDtypeStruct((s, D_MODEL), F32),
                   jax.ShapeDtypeStruct((s, D_MODEL), BF16)],
        scratch_shapes=[pltpu.VMEM((D_MODEL, X_DIM), BF16), pltpu.VMEM((X_DIM, D_MODEL), BF16)],
        compiler_params=pltpu.CompilerParams(
            dimension_semantics=("arbitrary",),
            vmem_limit_bytes=_vmem_limit(nbytes)),
        name="cross_attention",
    )(x, g_xattn.reshape(1, D_MODEL), g_mlp.reshape(1, D_MODEL), w_xq, kv, kv, w_xo)


CONV_END = 3 * CONV_DIM
ATTN_END = CONV_END + ATTN_DIM + 2 * KV_DIM + IDX_ALL
IW_END = ATTN_END + IDX_DIM + IDX_HEADS


def kernel(x, mem, rel_bias, norm_mix, w_in, conv_w, w_conv_out, w_attn_out, w_mix_out, norm_xattn, norm_mem, w_xq, w_xkv, w_xo, norm_mlp, w_up, w_down, norm_final):
    bsz, s, d = x.shape
    assert bsz == 1 and d == D_MODEL
    depth = w_in.shape[0]
    topk = min(TOPK_MAX, s // 4)
    xs = x.reshape(s, d)
    mems = mem.reshape(mem.shape[1], d)
    w_in_t = jnp.swapaxes(w_in, 1, 2)
    dx = None
    for l in range(depth):
        if dx is None:
            h = _rmsnorm(xs, norm_mix[l], BF16)
        else:
            xs, h = _add_rmsnorm(xs, dx, norm_mix[l], BF16, keep_sum=True)
        proj = functools.partial(_matmul, h, w_in_t, layer=l, b_transposed=True, tm=2048)
        a_in = _proj_gated_conv(h, w_in_t, l, conv_w[l])
        zb = proj(col0=CONV_END, n=ATTN_END - CONV_END, out_dtype=BF16, tn=512, name="proj_attn")
        zc = proj(col0=ATTN_END, n=LANES, out_dtype=F32, tn=LANES, name="proj_idx")
        zg = proj(col0=IW_END, n=2 * D_MODEL, out_dtype=F32, tn=512, name="proj_gate")
        o = _dsa_attention(zb, zc, rel_bias, topk)
        merged = _merge(a_in, o, w_conv_out, w_attn_out, l, zg, 0)
        xs = _matmul(merged, w_mix_out, layer=l, out_dtype=F32, tm=1024, tn=512,
                     epilogue="residual", residual=xs, name="mix_out")
        hm = _rmsnorm(mems, norm_mem[l], BF16)
        kv = _matmul(hm, w_xkv, layer=l, out_dtype=BF16, tm=256, tn=512, name="mem_kv")
        xs, hmlp = _cross_attention(xs, norm_xattn[l], norm_mlp[l], w_xq, kv, w_xo, l)
        act = _matmul(hmlp, w_up, layer=l, out_dtype=BF16, tm=2048, tn=512, epilogue="relu2", name="mlp_up")
        dx = _matmul(act, w_down, layer=l, out_dtype=F32, tm=2048, tn=1024, tk=1024, name="mlp_down")
    out = _add_rmsnorm(xs, dx, norm_final, F32, keep_sum=False)
    return out.reshape(bsz, s, d)
```

```python
import functools
import math

import numpy as np
import jax
import jax.numpy as jnp
from jax import lax
from jax.experimental import pallas as pl
from jax.experimental.pallas import tpu as pltpu

F32 = jnp.float32
BF16 = jnp.bfloat16

D_MODEL = 4096
CONV_DIM = 2048
CONV_K = 3
N_HEADS = 16
N_KV_HEADS = 4
HEAD_DIM = 128
GROUP = N_HEADS // N_KV_HEADS
ATTN_DIM = N_HEADS * HEAD_DIM
KV_DIM = N_KV_HEADS * HEAD_DIM
IDX_HEADS = 32
IDX_DIM = 64
IDX_ALL = IDX_HEADS * IDX_DIM
TOPK_MAX = 256
N_BUCKETS = 32
MAX_DISTANCE = 128
X_HEADS = 4
X_HEAD_DIM = 128
X_DIM = X_HEADS * X_HEAD_DIM
EPS = 1e-6

V7X_VMEM_BYTES = 64 * 1024 * 1024
LANES = 128
SUBLANES = 8

Q_TILE = 128
IDX_KEYS = 512
ATT_KEYS = 512
V_ROWS = HEAD_DIM + 16
BISECT_STEPS = 4
NEG_INF = float("-inf")
NT_DIMS = (((1,), (1,)), ((), ()))


INTERNAL_SCRATCH_BYTES = 8 << 20
VMEM_RESERVED_BYTES = 6 << 20


def _vmem_limit(nbytes):
    return int(min(nbytes + INTERNAL_SCRATCH_BYTES, V7X_VMEM_BYTES - VMEM_RESERVED_BYTES))


def _rmsnorm_kernel(x_ref, g_ref, o_ref):
    x = x_ref[...]
    inv = lax.rsqrt(jnp.mean(x * x, axis=-1, keepdims=True) + EPS)
    o_ref[...] = ((x * inv) * g_ref[...]).astype(o_ref.dtype)


def _rmsnorm(x, g, out_dtype, tr=256):
    rows, d = x.shape
    tr = min(tr, rows)
    return pl.pallas_call(
        _rmsnorm_kernel,
        grid=(rows // tr,),
        in_specs=[pl.BlockSpec((tr, d), lambda i: (i, 0)),
                  pl.BlockSpec((1, d), lambda i: (0, 0))],
        out_specs=pl.BlockSpec((tr, d), lambda i: (i, 0)),
        out_shape=jax.ShapeDtypeStruct((rows, d), out_dtype),
        compiler_params=pltpu.CompilerParams(
            dimension_semantics=("parallel",),
            vmem_limit_bytes=_vmem_limit(2 * tr * d * (4 + jnp.dtype(out_dtype).itemsize))),
        name="rmsnorm",
    )(x, g.reshape(1, d))


def _add_rmsnorm_kernel(x_ref, dx_ref, g_ref, *out_refs):
    x = x_ref[...] + dx_ref[...]
    inv = lax.rsqrt(jnp.mean(x * x, axis=-1, keepdims=True) + EPS)
    h_ref = out_refs[-1]
    h_ref[...] = ((x * inv) * g_ref[...]).astype(h_ref.dtype)
    if len(out_refs) == 2:
        out_refs[0][...] = x


def _add_rmsnorm(x, dx, g, out_dtype, keep_sum, tr=256):
    rows, d = x.shape
    tr = min(tr, rows)
    row_spec = pl.BlockSpec((tr, d), lambda i: (i, 0))
    out_specs = [row_spec, row_spec] if keep_sum else [row_spec]
    out_shape = [jax.ShapeDtypeStruct((rows, d), out_dtype)]
    if keep_sum:
        out_shape.insert(0, jax.ShapeDtypeStruct((rows, d), F32))
    outs = pl.pallas_call(
        _add_rmsnorm_kernel,
        grid=(rows // tr,),
        in_specs=[row_spec, row_spec, pl.BlockSpec((1, d), lambda i: (0, 0))],
        out_specs=out_specs,
        out_shape=out_shape,
        compiler_params=pltpu.CompilerParams(
            dimension_semantics=("parallel",),
            vmem_limit_bytes=_vmem_limit(2 * tr * d * (12 + jnp.dtype(out_dtype).itemsize))),
        name="add_rmsnorm",
    )(x, dx, g.reshape(1, d))
    return outs if keep_sum else outs[0]


def _mm_kernel(*refs, nk, epilogue, b_transposed):
    if epilogue == "residual":
        a_ref, b_ref, r_ref, o_ref = refs
    else:
        a_ref, b_ref, o_ref = refs
        r_ref = None

    def product():
        if b_transposed:
            return lax.dot_general(a_ref[...], b_ref[0].astype(BF16), NT_DIMS, preferred_element_type=F32)
        return jnp.dot(a_ref[...], b_ref[...].astype(BF16), preferred_element_type=F32)

    if nk > 1:
        @pl.when(pl.program_id(2) == 0)
        def _():
            o_ref[...] = jnp.zeros(o_ref.shape, o_ref.dtype)

        o_ref[...] += product()
        return
    acc = product()
    if epilogue == "relu2":
        acc = jnp.square(jnp.maximum(acc, 0.0))
    elif epilogue == "residual":
        acc = r_ref[...] + acc
    o_ref[...] = acc.astype(o_ref.dtype)


def _matmul(a, b, *, out_dtype, tm, tn, tk=None, layer=None, col0=0, n=None, b_transposed=False,
            epilogue=None, residual=None, name="matmul"):
    m, kdim = a.shape
    n_all = b.shape[-2] if b_transposed else b.shape[-1]
    n = (n_all - col0) if n is None else n
    tm, tn = min(tm, m), min(tn, n)
    tk = kdim if tk is None else tk
    nk = kdim // tk
    assert m % tm == 0 and n % tn == 0 and kdim % tk == 0
    if b_transposed:
        assert layer is not None and col0 % SUBLANES == 0
        b_spec = pl.BlockSpec((pl.Element(1), pl.Element(tn), pl.Element(tk)),
                              lambda i, j, k: (layer, pl.multiple_of(col0 + j * tn, SUBLANES), k * tk))
    else:
        assert col0 % tn == 0
        j0 = col0 // tn
        if layer is None:
            b_spec = pl.BlockSpec((tk, tn), lambda i, j, k: (k, j + j0))
        else:
            b_spec = pl.BlockSpec((None, tk, tn), lambda i, j, k: (layer, k, j + j0))
    a_buffers = 1 if (nk == 1 and tm * tk * 2 > V7X_VMEM_BYTES // 8) else 2
    in_specs = [pl.BlockSpec((tm, tk), lambda i, j, k: (i, k), pipeline_mode=pl.Buffered(a_buffers)), b_spec]
    args = [a, b]
    b_item = jnp.dtype(b.dtype).itemsize
    nbytes = a_buffers * tm * tk * 2 + 2 * tk * tn * b_item + 2 * tm * tn * jnp.dtype(out_dtype).itemsize
    if b_item != 2:
        nbytes += tk * tn * 2
    if epilogue == "residual":
        in_specs.append(pl.BlockSpec((tm, tn), lambda i, j, k: (i, j)))
        args.append(residual)
        nbytes += 2 * tm * tn * 4
    assert nk == 1 or (epilogue is None and out_dtype == F32)
    nbytes += tm * tn * 4
    return pl.pallas_call(
        functools.partial(_mm_kernel, nk=nk, epilogue=epilogue, b_transposed=b_transposed),
        grid=(m // tm, n // tn, nk),
        in_specs=in_specs,
        out_specs=pl.BlockSpec((tm, tn), lambda i, j, k: (i, j)),
        out_shape=jax.ShapeDtypeStruct((m, n), out_dtype),
        compiler_params=pltpu.CompilerParams(
            dimension_semantics=("parallel", "parallel", "arbitrary"),
            vmem_limit_bytes=_vmem_limit(nbytes)),
        name=name,
    )(*args)


def _proj_conv_kernel(h_ref, wu_ref, wcb_ref, wcc_ref, cw_ref, o_ref, wb_ref, p_ref, *, tm):
    @pl.when(pl.program_id(1) == 0)
    def _():
        wb_ref[0] = wu_ref[0].astype(BF16)
        wb_ref[1] = wcb_ref[0].astype(BF16)
        wb_ref[2] = wcc_ref[0].astype(BF16)
        p_ref[0:SUBLANES, :] = jnp.zeros((SUBLANES, p_ref.shape[1]), F32)

    h = h_ref[...]
    u = lax.dot_general(h, wb_ref[0], NT_DIMS, preferred_element_type=F32)
    cb = lax.dot_general(h, wb_ref[1], NT_DIMS, preferred_element_type=F32)
    cc = lax.dot_general(h, wb_ref[2], NT_DIMS, preferred_element_type=F32)
    p = cc * u
    p_ref[SUBLANES:SUBLANES + tm, :] = p
    p1 = p_ref[SUBLANES - 1:SUBLANES - 1 + tm, :]
    p2 = p_ref[SUBLANES - 2:SUBLANES - 2 + tm, :]
    w = cw_ref[...]
    z = w[2:3, :] * p + w[0:1, :] * p2 + w[1:2, :] * p1
    o_ref[...] = (cb * z).astype(o_ref.dtype)
    p_ref[0:SUBLANES, :] = p_ref[tm:tm + SUBLANES, :]


def _proj_gated_conv(h, w_in_t, layer, conv_w, tm=1024, tc=256):
    s, d = h.shape
    tm = min(tm, s)

    def w_spec(row0):
        return pl.BlockSpec((pl.Element(1), pl.Element(tc), pl.Element(d)),
                            lambda c, i: (layer, pl.multiple_of(row0 + c * tc, SUBLANES), 0))

    nbytes = 2 * tm * d * 2 + 3 * 2 * tc * d * 4 + 3 * tc * d * 2 + 2 * tm * tc * 2 + 6 * tm * tc * 4
    return pl.pallas_call(
        functools.partial(_proj_conv_kernel, tm=tm),
        grid=(CONV_DIM // tc, s // tm),
        in_specs=[pl.BlockSpec((tm, d), lambda c, i: (i, 0)),
                  w_spec(0), w_spec(CONV_DIM), w_spec(2 * CONV_DIM),
                  pl.BlockSpec((CONV_K, tc), lambda c, i: (0, c))],
        out_specs=pl.BlockSpec((tm, tc), lambda c, i: (i, c)),
        out_shape=jax.ShapeDtypeStruct((s, CONV_DIM), BF16),
        scratch_shapes=[pltpu.VMEM((3, tc, d), BF16), pltpu.VMEM((tm + SUBLANES, tc), F32)],
        compiler_params=pltpu.CompilerParams(
            dimension_semantics=("parallel", "arbitrary"),
            vmem_limit_bytes=_vmem_limit(nbytes)),
        name="proj_conv",
    )(h, w_in_t, w_in_t, w_in_t, conv_w)


def _t5_bucket_np(dist):
    n = np.maximum(dist, 0)
    max_exact = N_BUCKETS // 2
    nf = np.maximum(n, 1).astype(np.float64)
    large = max_exact + (np.log(nf / max_exact) / math.log(MAX_DISTANCE / max_exact)
                         * (N_BUCKETS - max_exact)).astype(np.int32)
    large = np.minimum(large, N_BUCKETS - 1)
    return np.where(n < max_exact, n, large).astype(np.int32)


def _near_bias(rel_bias):
    sl = np.arange(Q_TILE)[:, None]
    tl = np.arange(Q_TILE)[None, :]
    dist = np.stack([tl - sl + Q_TILE, tl - sl])
    assert _t5_bucket_np(np.array([Q_TILE]))[0] == N_BUCKETS - 1
    bucket = _t5_bucket_np(dist)
    rel = (rel_bias - rel_bias[N_BUCKETS - 1][None, :]) * (HEAD_DIM ** 0.5)
    onehot = np.eye(N_BUCKETS, dtype=np.float32)[bucket.reshape(-1)]
    b = jnp.dot(onehot, rel, precision=lax.Precision.HIGHEST)
    b = b.reshape(2, Q_TILE, Q_TILE, N_KV_HEADS, GROUP)
    b = b.transpose(0, 3, 1, 4, 2)
    b = b.reshape(2, N_KV_HEADS, Q_TILE, GROUP * Q_TILE).astype(F32)
    return jnp.concatenate([jnp.zeros_like(b[:1]), b], axis=0)


def _dsa_kernel(q_ref, k_ref, iq_lo_ref, iq_hi_ref, vt_ref, ik2_ref, w_ref, bias_ref, o_ref,
                sc_ref, s_ref, acc_ref, m_ref, *, topk, max_iters, pos_bits):
    i = pl.program_id(0)
    q0 = i * Q_TILE
    kf = float(topk)
    per = ATT_KEYS // Q_TILE
    c_last = i // per
    n_cnt = c_last + 1
    n_idx = (q0 + Q_TILE + IDX_KEYS - 1) // IDX_KEYS
    qpos = q0 + lax.broadcasted_iota(jnp.int32, (1, Q_TILE), 1)

    def fold8(x, op):
        rows = x.shape[0]
        if rows > SUBLANES * SUBLANES:
            x = op(x.reshape(SUBLANES, rows // SUBLANES, x.shape[1]), axis=0)
        return op(x.reshape(x.shape[0] // SUBLANES, SUBLANES, x.shape[1]), axis=0)

    def col_reduce(x, op):
        return op(fold8(x, op), axis=0, keepdims=True)

    n_dots = IDX_ALL // (2 * LANES)
    half = n_dots // 2

    def head_pairs(d):
        ref, e = (iq_lo_ref, d) if d < half else (iq_hi_ref, d - half)
        return jnp.concatenate([ref[:, (2 * e) * LANES:(2 * e + 1) * LANES],
                                ref[:, (2 * e + 1) * LANES:(2 * e + 2) * LANES]], axis=0)

    rhs = [head_pairs(d) for d in range(n_dots)]

    def idx_body(j, carry):
        mx, mn = carry
        s0 = pl.multiple_of(j * IDX_KEYS, IDX_KEYS)
        lhs = jnp.concatenate([ik2_ref[pl.ds(s0, IDX_KEYS), 0:LANES],
                               ik2_ref[pl.ds(s0, IDX_KEYS), LANES:2 * LANES]], axis=0)
        acc = None
        for d in range(n_dots):
            r = lax.dot_general(lhs, rhs[d], NT_DIMS, preferred_element_type=F32)
            r = jnp.maximum(r, 0.0)
            t = r[:IDX_KEYS] * w_ref[2 * d:2 * d + 1, :] + r[IDX_KEYS:] * w_ref[2 * d + 1:2 * d + 2, :]
            acc = t if acc is None else acc + t
        acc = acc[:, :Q_TILE] + acc[:, Q_TILE:]
        kpos = s0 + lax.broadcasted_iota(jnp.int32, (IDX_KEYS, 1), 0)
        causal = kpos <= qpos
        lo_fill = jnp.where(causal, acc, NEG_INF)
        sc_ref[pl.ds(s0, IDX_KEYS), :] = lo_fill
        mx = jnp.maximum(mx, fold8(lo_fill, jnp.max))
        mn = jnp.minimum(mn, fold8(jnp.where(causal, acc, -NEG_INF), jnp.min))
        return mx, mn

    mx8, mn8 = lax.fori_loop(
        0, n_idx, idx_body,
        (jnp.full((SUBLANES, Q_TILE), NEG_INF, F32), jnp.full((SUBLANES, Q_TILE), -NEG_INF, F32)))
    hi0 = jnp.max(mx8, axis=0, keepdims=True)
    lo0 = jnp.min(mn8, axis=0, keepdims=True)

    def fill_body(j, carry):
        sc_ref[pl.ds(pl.multiple_of(j * IDX_KEYS, IDX_KEYS), IDX_KEYS), :] = jnp.full(
            (IDX_KEYS, Q_TILE), NEG_INF, F32)
        return carry

    lax.fori_loop(n_idx, n_cnt * (ATT_KEYS // IDX_KEYS), fill_body, 0)

    def key_pos(c):
        return c * ATT_KEYS + lax.broadcasted_iota(jnp.int32, (ATT_KEYS, 1), 0)

    def count_rows(pred):
        def body(c, cnt):
            blk = sc_ref[pl.ds(pl.multiple_of(c * ATT_KEYS, ATT_KEYS), ATT_KEYS), :]
            return cnt + fold8(pred(blk, key_pos(c)), jnp.sum)
        c8 = lax.fori_loop(0, n_cnt, body, jnp.zeros((SUBLANES, Q_TILE), F32))
        return jnp.sum(c8, axis=0, keepdims=True)

    def count_ge(tau):
        return count_rows(lambda blk, pos: jnp.where(blk >= tau, 1.0, 0.0))

    def n_active(lo, hi, flo):
        mid = 0.5 * lo + 0.5 * hi
        act = jnp.logical_and(flo > kf, jnp.logical_and(mid > lo, mid < hi))
        return act, jnp.sum(jnp.where(act, 1.0, 0.0))

    flo0 = (qpos + 1).astype(F32)

    def bis_cond(st):
        _, _, _, n, it = st
        return jnp.logical_and(n > 0.0, it < max_iters)

    def bis_body(st):
        lo, hi, flo, _, it = st
        for _ in range(BISECT_STEPS):
            act, _ = n_active(lo, hi, flo)
            mid = 0.5 * lo + 0.5 * hi
            c = count_ge(mid)
            up = jnp.logical_and(act, c >= kf)
            dn = jnp.logical_and(act, c < kf)
            lo = jnp.where(up, mid, lo)
            flo = jnp.where(up, c, flo)
            hi = jnp.where(dn, mid, hi)
        _, n = n_active(lo, hi, flo)
        return lo, hi, flo, n, it + 1

    _, n0 = n_active(lo0, hi0, flo0)
    lo, hi, flo, _, _ = lax.while_loop(bis_cond, bis_body, (lo0, hi0, flo0, n0, jnp.int32(0)))
    open_hi = jnp.logical_and(flo > kf, hi == hi0)
    n_open = jnp.sum(jnp.where(open_hi, 1.0, 0.0))
    c_hi = lax.cond(n_open > 0.0, lambda: count_ge(hi), lambda: jnp.zeros((1, Q_TILE), F32))
    use_hi = jnp.logical_and(open_hi, c_hi >= kf)
    tau = jnp.where(use_hi, hi, lo)
    n_ge = jnp.where(use_hi, c_hi, flo)

    tied = n_ge > kf
    n_keys = n_cnt * ATT_KEYS

    def tie_cut():
        n_gt = count_rows(lambda blk, pos: jnp.where(blk > tau, 1.0, 0.0))
        need = kf - n_gt

        def step(_, st):
            lo_p, hi_p = st
            mid = lax.shift_right_logical(lo_p + hi_p, 1)
            below = count_rows(
                lambda blk, pos: jnp.where(blk == tau, jnp.where(pos < mid, 1.0, 0.0), 0.0))
            ge = below >= need
            return jnp.where(ge, lo_p, mid), jnp.where(ge, mid, hi_p)

        lo_p = jnp.zeros((1, Q_TILE), jnp.int32)
        hi_p = jnp.zeros((1, Q_TILE), jnp.int32) + n_keys
        _, hi_p = lax.fori_loop(0, pos_bits, step, (lo_p, hi_p))
        return hi_p

    n_tied = jnp.sum(jnp.where(tied, 1.0, 0.0))
    cut = lax.cond(n_tied > 0.0, tie_cut, lambda: jnp.zeros((1, Q_TILE), jnp.int32) + n_keys)
    cut = jnp.where(tied, cut, n_keys)

    def mask_body(c, carry):
        rows = pl.ds(pl.multiple_of(c * ATT_KEYS, ATT_KEYS), ATT_KEYS)
        blk = sc_ref[rows, :]
        at_tau = jnp.where(key_pos(c) < cut, 0.0, NEG_INF)
        sc_ref[rows, :] = jnp.where(blk > tau, 0.0, jnp.where(blk == tau, at_tau, NEG_INF))
        return carry

    lax.fori_loop(0, n_cnt, mask_body, 0)

    c2 = (HEAD_DIM ** -0.5) * math.log2(math.e)
    qg = []
    for g in range(N_KV_HEADS):
        qg.append(jnp.concatenate(
            [q_ref[:, (g * GROUP + hh) * HEAD_DIM:(g * GROUP + hh + 1) * HEAD_DIM] for hh in range(GROUP)],
            axis=0))

    m_ref[...] = jnp.full(m_ref.shape, NEG_INF, F32)
    acc_ref[...] = jnp.zeros(acc_ref.shape, F32)

    def logits(c, g, slot):
        s0 = pl.multiple_of(jnp.minimum(c, n_cnt - 1) * ATT_KEYS, ATT_KEYS)
        kb = k_ref[pl.ds(s0, ATT_KEYS), g * HEAD_DIM:(g + 1) * HEAD_DIM]
        s_ref[slot] = lax.dot_general(kb, qg[g], NT_DIMS, preferred_element_type=F32)

    def chunk(c, near):
        s0 = pl.multiple_of(c * ATT_KEYS, ATT_KEYS)
        madd = jnp.concatenate([sc_ref[pl.ds(s0, ATT_KEYS), :]] * GROUP, axis=1)
        for g in range(N_KV_HEADS):
            if g + 1 < N_KV_HEADS:
                logits(c, g + 1, (g + 1) % 2)
            else:
                logits(c + 1, 0, 0)
            vtb = vt_ref[g, :, pl.ds(s0, ATT_KEYS)]
            s = s_ref[g % 2] + madd
            if near:
                rows = []
                for r in range(per):
                    d = i - (c * per + r)
                    rows.append(bias_ref[jnp.where(d == 1, 1, jnp.where(d == 0, 2, 0)), g])
                s = s + jnp.concatenate(rows, axis=0)
            m_old = m_ref[g]
            m_new = jnp.maximum(m_old, col_reduce(s, jnp.max))
            m_use = jnp.where(m_new == NEG_INF, 0.0, m_new)
            alpha = jnp.exp2(c2 * (m_old - m_use))
            p = jnp.exp2(c2 * (s - m_use))
            pv = jnp.dot(vtb, p.astype(BF16), preferred_element_type=F32)
            acc_ref[g] = alpha * acc_ref[g] + pv
            m_ref[g] = m_new

    n_far = jnp.where(i % per == 0, jnp.maximum(c_last - 1, 0), c_last)
    logits(0, 0, 0)

    def far_body(c, carry):
        chunk(c, False)
        return carry

    lax.fori_loop(0, n_far, far_body, 0)

    def near_body(c, carry):
        chunk(c, True)
        return carry

    lax.fori_loop(n_far, n_cnt, near_body, 0)

    for g in range(N_KV_HEADS):
        out = acc_ref[g, 0:HEAD_DIM, :] / acc_ref[g, HEAD_DIM:HEAD_DIM + 1, :]
        for hh in range(GROUP):
            h = g * GROUP + hh
            o_ref[:, h * HEAD_DIM:(h + 1) * HEAD_DIM] = (
                out[:, hh * Q_TILE:(hh + 1) * Q_TILE].T.astype(o_ref.dtype))


def _dsa_attention(zb, zc, rel_bias, topk):
    s = zb.shape[0]
    nq = s // Q_TILE
    assert s % ATT_KEYS == 0
    k_col = ATTN_DIM // KV_DIM
    iq_half = IDX_ALL // 2
    iq_col = (ATTN_DIM + 2 * KV_DIM) // iq_half
    assert (ATTN_DIM + 2 * KV_DIM) % iq_half == 0
    vt = zb[:, ATTN_DIM + KV_DIM:ATTN_DIM + 2 * KV_DIM].T.reshape(N_KV_HEADS, HEAD_DIM, s)
    vt = jnp.concatenate([vt, jnp.ones((N_KV_HEADS, V_ROWS - HEAD_DIM, s), BF16)], axis=1)
    ik = zc[:, :IDX_DIM].astype(BF16)
    ik2 = jnp.concatenate([ik, jnp.zeros((s, 2 * IDX_DIM), BF16), ik], axis=1)
    w = zc[:, IDX_DIM:IDX_DIM + IDX_HEADS] * (IDX_HEADS ** -0.5) * (IDX_DIM ** -0.5)
    w = w.reshape(nq, Q_TILE, IDX_HEADS // 4, 2, 2).transpose(0, 2, 4, 3, 1)
    w = w.reshape(nq, IDX_HEADS // 2, 2 * Q_TILE)
    bias = _near_bias(rel_bias)

    resident = dict(pipeline_mode=pl.Buffered(1))
    nbytes = (s * KV_DIM * 2 + vt.size * 2 + s * 2 * LANES * 2 + bias.size * 4 + s * Q_TILE * 4
              + 2 * (2 * Q_TILE * ATTN_DIM * 2 * 2 + IDX_HEADS * Q_TILE * 4)
              + 6 * ATT_KEYS * GROUP * Q_TILE * 4 + 3 * N_KV_HEADS * V_ROWS * GROUP * Q_TILE * 4)
    return pl.pallas_call(
        functools.partial(_dsa_kernel, topk=topk, max_iters=400, pos_bits=int(s).bit_length()),
        grid=(nq,),
        in_specs=[pl.BlockSpec((Q_TILE, ATTN_DIM), lambda i: (i, 0)),
                  pl.BlockSpec((s, KV_DIM), lambda i: (0, k_col), **resident),
                  pl.BlockSpec((Q_TILE, iq_half), lambda i: (i, iq_col)),
                  pl.BlockSpec((Q_TILE, iq_half), lambda i: (i, iq_col + 1)),
                  pl.BlockSpec(vt.shape, lambda i: (0, 0, 0), **resident),
                  pl.BlockSpec((s, 2 * LANES), lambda i: (0, 0), **resident),
                  pl.BlockSpec((None, IDX_HEADS // 2, 2 * Q_TILE), lambda i: (i, 0, 0)),
                  pl.BlockSpec(bias.shape, lambda i: (0, 0, 0, 0), **resident)],
        out_specs=pl.BlockSpec((Q_TILE, ATTN_DIM), lambda i: (i, 0)),
        out_shape=jax.ShapeDtypeStruct((s, ATTN_DIM), BF16),
        scratch_shapes=[pltpu.VMEM((s, Q_TILE), F32),
                        pltpu.VMEM((2, ATT_KEYS, GROUP * Q_TILE), F32),
                        pltpu.VMEM((N_KV_HEADS, V_ROWS, GROUP * Q_TILE), F32),
                        pltpu.VMEM((N_KV_HEADS, 1, GROUP * Q_TILE), F32)],
        compiler_params=pltpu.CompilerParams(
            dimension_semantics=("arbitrary",),
            vmem_limit_bytes=_vmem_limit(nbytes)),
        name="dsa_attention",
    )(zb, zb, zb, zb, vt, ik2, w, bias)


def _merge_kernel(a_ref, o_ref, wa_ref, wo_ref, ga_ref, gb_ref, out_ref, wab_ref, wob_ref):
    @pl.when(pl.program_id(1) == 0)
    def _():
        wab_ref[...] = wa_ref[...].astype(BF16)
        wob_ref[...] = wo_ref[...].astype(BF16)

    ya = jnp.dot(a_ref[...], wab_ref[...], preferred_element_type=F32)
    yb = jnp.dot(o_ref[...], wob_ref[...], preferred_element_type=F32)
    merged = jax.nn.sigmoid(ga_ref[...]) * ya + jax.nn.sigmoid(gb_ref[...]) * yb
    out_ref[...] = merged.astype(out_ref.dtype)


def _merge(a_in, o, w_conv_out, w_attn_out, layer, zg, g_col0, tm=512, tn=1024):
    s = a_in.shape[0]
    tm = min(tm, s)
    nn = D_MODEL // tn
    assert g_col0 % tn == 0
    gj = g_col0 // tn
    once = dict(pipeline_mode=pl.Buffered(1))
    nbytes = (2 * (2 * tm * CONV_DIM * 2 + 2 * tm * tn * 4 + tm * tn * 2) + 2 * CONV_DIM * tn * 4
              + 2 * CONV_DIM * tn * 2 + 4 * tm * tn * 4)
    return pl.pallas_call(
        _merge_kernel,
        grid=(nn, s // tm),
        in_specs=[pl.BlockSpec((tm, CONV_DIM), lambda j, i: (i, 0)),
                  pl.BlockSpec((tm, ATTN_DIM), lambda j, i: (i, 0)),
                  pl.BlockSpec((None, CONV_DIM, tn), lambda j, i: (layer, 0, j), **once),
                  pl.BlockSpec((None, ATTN_DIM, tn), lambda j, i: (layer, 0, j), **once),
                  pl.BlockSpec((tm, tn), lambda j, i: (i, j + gj)),
                  pl.BlockSpec((tm, tn), lambda j, i: (i, j + gj + nn))],
        out_specs=pl.BlockSpec((tm, tn), lambda j, i: (i, j)),
        out_shape=jax.ShapeDtypeStruct((s, D_MODEL), BF16),
        scratch_shapes=[pltpu.VMEM((CONV_DIM, tn), BF16), pltpu.VMEM((ATTN_DIM, tn), BF16)],
        compiler_params=pltpu.CompilerParams(
            dimension_semantics=("parallel", "arbitrary"),
            vmem_limit_bytes=_vmem_limit(nbytes)),
        name="merge",
    )(a_in, o, w_conv_out, w_attn_out, zg, zg)


def _xattn_kernel(x_ref, gx_ref, gm_ref, wq_ref, kx_ref, vx_ref, wo_ref, out_ref, h_ref, wqb_ref, wob_ref):
    @pl.when(pl.program_id(0) == 0)
    def _():
        wqb_ref[...] = wq_ref[...].astype(BF16)
        wob_ref[...] = wo_ref[...].astype(BF16)

    def rms(v, g_ref):
        inv = lax.rsqrt(jnp.mean(v * v, axis=-1, keepdims=True) + EPS)
        return ((v * inv) * g_ref[...]).astype(BF16)

    x = x_ref[...]
    qx = jnp.dot(rms(x, gx_ref), wqb_ref[...], preferred_element_type=F32).astype(BF16)
    scale = X_HEAD_DIM ** -0.5
    outs = []
    for h in range(X_HEADS):
        sl = slice(h * X_HEAD_DIM, (h + 1) * X_HEAD_DIM)
        s = lax.dot_general(qx[:, sl], kx_ref[:, sl], NT_DIMS, preferred_element_type=F32) * scale
        m = jnp.max(s, axis=-1, keepdims=True)
        p = jnp.exp(s - m)
        l = jnp.sum(p, axis=-1, keepdims=True)
        oh = jnp.dot(p.astype(BF16), vx_ref[:, sl], preferred_element_type=F32) / l
        outs.append(oh.astype(BF16))
    o = jnp.concatenate(outs, axis=1)
    x_new = x + jnp.dot(o, wob_ref[...], preferred_element_type=F32)
    out_ref[...] = x_new
    h_ref[...] = rms(x_new, gm_ref)


def _cross_attention(x, g_xattn, g_mlp, w_xq, kv, w_xo, layer, tm=256):
    s = x.shape[0]
    tm = min(tm, s)
    n_mem = kv.shape[0]
    resident = dict(pipeline_mode=pl.Buffered(1))
    nbytes = (2 * (2 * tm * D_MODEL * 4 + tm * D_MODEL * 2) + 2 * D_MODEL * X_DIM * (4 + 2)
              + 2 * n_mem * X_DIM * 2 + 4 * tm * D_MODEL * 4)
    return pl.pallas_call(
        _xattn_kernel,
        grid=(s // tm,),
        in_specs=[pl.BlockSpec((tm, D_MODEL), lambda i: (i, 0)),
                  pl.BlockSpec((1, D_MODEL), lambda i: (0, 0)),
                  pl.BlockSpec((1, D_MODEL), lambda i: (0, 0)),
                  pl.BlockSpec((None, D_MODEL, X_DIM), lambda i: (layer, 0, 0), **resident),
                  pl.BlockSpec((n_mem, X_DIM), lambda i: (0, 0), **resident),
                  pl.BlockSpec((n_mem, X_DIM), lambda i: (0, 1), **resident),
                  pl.BlockSpec((None, X_DIM, D_MODEL), lambda i: (layer, 0, 0), **resident)],
        out_specs=[pl.BlockSpec((tm, D_MODEL), lambda i: (i, 0)),
                   pl.BlockSpec((tm, D_MODEL), lambda i: (i, 0))],
        out_shape=[jax.ShapeDtypeStruct((s, D_MODEL), F32),
                   jax.ShapeDtypeStruct((s, D_MODEL), BF16)],
        scratch_shapes=[pltpu.VMEM((D_MODEL, X_DIM), BF16), pltpu.VMEM((X_DIM, D_MODEL), BF16)],
        compiler_params=pltpu.CompilerParams(
            dimension_semantics=("arbitrary",),
            vmem_limit_bytes=_vmem_limit(nbytes)),
        name="cross_attention",
    )(x, g_xattn.reshape(1, D_MODEL), g_mlp.reshape(1, D_MODEL), w_xq, kv, kv, w_xo)


CONV_END = 3 * CONV_DIM
ATTN_END = CONV_END + ATTN_DIM + 2 * KV_DIM + IDX_ALL
IW_END = ATTN_END + IDX_DIM + IDX_HEADS


def kernel(x, mem, rel_bias, norm_mix, w_in, conv_w, w_conv_out, w_attn_out, w_mix_out, norm_xattn, norm_mem, w_xq, w_xkv, w_xo, norm_mlp, w_up, w_down, norm_final):
    bsz, s, d = x.shape
    assert bsz == 1 and d == D_MODEL
    depth = w_in.shape[0]
    topk = min(TOPK_MAX, s // 4)
    xs = x.reshape(s, d)
    mems = mem.reshape(mem.shape[1], d)
    w_in_t = jnp.swapaxes(w_in, 1, 2)
    dx = None
    for l in range(depth):
        if dx is None:
            h = _rmsnorm(xs, norm_mix[l], BF16)
        else:
            xs, h = _add_rmsnorm(xs, dx, norm_mix[l], BF16, keep_sum=True)
        proj = functools.partial(_matmul, h, w_in_t, layer=l, b_transposed=True, tm=2048)
        a_in = _proj_gated_conv(h, w_in_t, l, conv_w[l])
        zb = proj(col0=CONV_END, n=ATTN_END - CONV_END, out_dtype=BF16, tn=512, name="proj_attn")
        zc = _matmul(h, w_in_t, layer=l, b_transposed=True, col0=ATTN_END, n=LANES, out_dtype=F32,
                     tm=1024, tn=LANES, name="proj_idx")
        zg = proj(col0=IW_END, n=2 * D_MODEL, out_dtype=F32, tn=512, name="proj_gate")
        o = _dsa_attention(zb, zc, rel_bias, topk)
        merged = _merge(a_in, o, w_conv_out, w_attn_out, l, zg, 0)
        xs = _matmul(merged, w_mix_out, layer=l, out_dtype=F32, tm=1024, tn=512,
                     epilogue="residual", residual=xs, name="mix_out")
        hm = _rmsnorm(mems, norm_mem[l], BF16)
        kv = _matmul(hm, w_xkv, layer=l, out_dtype=BF16, tm=256, tn=512, name="mem_kv")
        xs, hmlp = _cross_attention(xs, norm_xattn[l], norm_mlp[l], w_xq, kv, w_xo, l)
        act = _matmul(hmlp, w_up, layer=l, out_dtype=BF16, tm=2048, tn=512, epilogue="relu2", name="mlp_up")
        dx = _matmul(act, w_down, layer=l, out_dtype=F32, tm=2048, tn=1024, tk=1024, name="mlp_down")
    out = _add_rmsnorm(xs, dx, norm_final, F32, keep_sum=False)
    return out.reshape(bsz, s, d)
```

```python
import functools
import math

import numpy as np
import jax
import jax.numpy as jnp
from jax import lax
from jax.experimental import pallas as pl
from jax.experimental.pallas import tpu as pltpu

F32 = jnp.float32
BF16 = jnp.bfloat16

D_MODEL = 4096
CONV_DIM = 2048
CONV_K = 3
N_HEADS = 16
N_KV_HEADS = 4
HEAD_DIM = 128
GROUP = N_HEADS // N_KV_HEADS
ATTN_DIM = N_HEADS * HEAD_DIM
KV_DIM = N_KV_HEADS * HEAD_DIM
IDX_HEADS = 32
IDX_DIM = 64
IDX_ALL = IDX_HEADS * IDX_DIM
TOPK_MAX = 256
N_BUCKETS = 32
MAX_DISTANCE = 128
X_HEADS = 4
X_HEAD_DIM = 128
X_DIM = X_HEADS * X_HEAD_DIM
EPS = 1e-6

V7X_VMEM_BYTES = 64 * 1024 * 1024
LANES = 128
SUBLANES = 8

Q_TILE = 128
IDX_KEYS = 512
ATT_KEYS = 512
V_ROWS = HEAD_DIM + 16
BISECT_STEPS = 4
NEG_INF = float("-inf")
NT_DIMS = (((1,), (1,)), ((), ()))


INTERNAL_SCRATCH_BYTES = 8 << 20
VMEM_RESERVED_BYTES = 6 << 20


def _vmem_limit(nbytes):
    return int(min(nbytes + INTERNAL_SCRATCH_BYTES, V7X_VMEM_BYTES - VMEM_RESERVED_BYTES))


def _rmsnorm_kernel(x_ref, g_ref, o_ref):
    x = x_ref[...]
    inv = lax.rsqrt(jnp.mean(x * x, axis=-1, keepdims=True) + EPS)
    o_ref[...] = ((x * inv) * g_ref[...]).astype(o_ref.dtype)


def _rmsnorm(x, g, out_dtype, tr=256):
    rows, d = x.shape
    tr = min(tr, rows)
    return pl.pallas_call(
        _rmsnorm_kernel,
        grid=(rows // tr,),
        in_specs=[pl.BlockSpec((tr, d), lambda i: (i, 0)),
                  pl.BlockSpec((1, d), lambda i: (0, 0))],
        out_specs=pl.BlockSpec((tr, d), lambda i: (i, 0)),
        out_shape=jax.ShapeDtypeStruct((rows, d), out_dtype),
        compiler_params=pltpu.CompilerParams(
            dimension_semantics=("parallel",),
            vmem_limit_bytes=_vmem_limit(2 * tr * d * (4 + jnp.dtype(out_dtype).itemsize))),
        name="rmsnorm",
    )(x, g.reshape(1, d))


def _add_rmsnorm_kernel(x_ref, dx_ref, g_ref, *out_refs):
    x = x_ref[...] + dx_ref[...]
    inv = lax.rsqrt(jnp.mean(x * x, axis=-1, keepdims=True) + EPS)
    h_ref = out_refs[-1]
    h_ref[...] = ((x * inv) * g_ref[...]).astype(h_ref.dtype)
    if len(out_refs) == 2:
        out_refs[0][...] = x


def _add_rmsnorm(x, dx, g, out_dtype, keep_sum, tr=256):
    rows, d = x.shape
    tr = min(tr, rows)
    row_spec = pl.BlockSpec((tr, d), lambda i: (i, 0))
    out_specs = [row_spec, row_spec] if keep_sum else [row_spec]
    out_shape = [jax.ShapeDtypeStruct((rows, d), out_dtype)]
    if keep_sum:
        out_shape.insert(0, jax.ShapeDtypeStruct((rows, d), F32))
    outs = pl.pallas_call(
        _add_rmsnorm_kernel,
        grid=(rows // tr,),
        in_specs=[row_spec, row_spec, pl.BlockSpec((1, d), lambda i: (0, 0))],
        out_specs=out_specs,
        out_shape=out_shape,
        compiler_params=pltpu.CompilerParams(
            dimension_semantics=("parallel",),
            vmem_limit_bytes=_vmem_limit(2 * tr * d * (12 + jnp.dtype(out_dtype).itemsize))),
        name="add_rmsnorm",
    )(x, dx, g.reshape(1, d))
    return outs if keep_sum else outs[0]


def _mm_kernel(*refs, nk, epilogue, b_transposed):
    if epilogue == "residual":
        a_ref, b_ref, r_ref, o_ref = refs
    else:
        a_ref, b_ref, o_ref = refs
        r_ref = None

    def product():
        if b_transposed:
            return lax.dot_general(a_ref[...], b_ref[0].astype(BF16), NT_DIMS, preferred_element_type=F32)
        return jnp.dot(a_ref[...], b_ref[...].astype(BF16), preferred_element_type=F32)

    if nk > 1:
        @pl.when(pl.program_id(2) == 0)
        def _():
            o_ref[...] = jnp.zeros(o_ref.shape, o_ref.dtype)

        o_ref[...] += product()
        return
    acc = product()
    if epilogue == "relu2":
        acc = jnp.square(jnp.maximum(acc, 0.0))
    elif epilogue == "residual":
        acc = r_ref[...] + acc
    o_ref[...] = acc.astype(o_ref.dtype)


def _matmul(a, b, *, out_dtype, tm, tn, tk=None, layer=None, col0=0, n=None, b_transposed=False,
            epilogue=None, residual=None, name="matmul"):
    m, kdim = a.shape
    n_all = b.shape[-2] if b_transposed else b.shape[-1]
    n = (n_all - col0) if n is None else n
    tm, tn = min(tm, m), min(tn, n)
    tk = kdim if tk is None else tk
    nk = kdim // tk
    assert m % tm == 0 and n % tn == 0 and kdim % tk == 0
    if b_transposed:
        assert layer is not None and col0 % SUBLANES == 0
        b_spec = pl.BlockSpec((pl.Element(1), pl.Element(tn), pl.Element(tk)),
                              lambda i, j, k: (layer, pl.multiple_of(col0 + j * tn, SUBLANES), k * tk))
    else:
        assert col0 % tn == 0
        j0 = col0 // tn
        if layer is None:
            b_spec = pl.BlockSpec((tk, tn), lambda i, j, k: (k, j + j0))
        else:
            b_spec = pl.BlockSpec((None, tk, tn), lambda i, j, k: (layer, k, j + j0))
    a_buffers = 1 if (nk == 1 and tm * tk * 2 > V7X_VMEM_BYTES // 8) else 2
    in_specs = [pl.BlockSpec((tm, tk), lambda i, j, k: (i, k), pipeline_mode=pl.Buffered(a_buffers)), b_spec]
    args = [a, b]
    b_item = jnp.dtype(b.dtype).itemsize
    nbytes = a_buffers * tm * tk * 2 + 2 * tk * tn * b_item + 2 * tm * tn * jnp.dtype(out_dtype).itemsize
    if b_item != 2:
        nbytes += tk * tn * 2
    if epilogue == "residual":
        in_specs.append(pl.BlockSpec((tm, tn), lambda i, j, k: (i, j)))
        args.append(residual)
        nbytes += 2 * tm * tn * 4
    assert nk == 1 or (epilogue is None and out_dtype == F32)
    nbytes += tm * tn * 4
    return pl.pallas_call(
        functools.partial(_mm_kernel, nk=nk, epilogue=epilogue, b_transposed=b_transposed),
        grid=(m // tm, n // tn, nk),
        in_specs=in_specs,
        out_specs=pl.BlockSpec((tm, tn), lambda i, j, k: (i, j)),
        out_shape=jax.ShapeDtypeStruct((m, n), out_dtype),
        compiler_params=pltpu.CompilerParams(
            dimension_semantics=("parallel", "parallel", "arbitrary"),
            vmem_limit_bytes=_vmem_limit(nbytes)),
        name=name,
    )(*args)


def _proj_conv_kernel(h_ref, wu_ref, wcb_ref, wcc_ref, cw_ref, o_ref, wb_ref, p_ref, *, tm):
    @pl.when(pl.program_id(1) == 0)
    def _():
        wb_ref[0] = wu_ref[0].astype(BF16)
        wb_ref[1] = wcb_ref[0].astype(BF16)
        wb_ref[2] = wcc_ref[0].astype(BF16)
        p_ref[0:SUBLANES, :] = jnp.zeros((SUBLANES, p_ref.shape[1]), F32)

    h = h_ref[...]
    u = lax.dot_general(h, wb_ref[0], NT_DIMS, preferred_element_type=F32)
    cb = lax.dot_general(h, wb_ref[1], NT_DIMS, preferred_element_type=F32)
    cc = lax.dot_general(h, wb_ref[2], NT_DIMS, preferred_element_type=F32)
    p = cc * u
    p_ref[SUBLANES:SUBLANES + tm, :] = p
    p1 = p_ref[SUBLANES - 1:SUBLANES - 1 + tm, :]
    p2 = p_ref[SUBLANES - 2:SUBLANES - 2 + tm, :]
    w = cw_ref[...]
    z = w[2:3, :] * p + w[0:1, :] * p2 + w[1:2, :] * p1
    o_ref[...] = (cb * z).astype(o_ref.dtype)
    p_ref[0:SUBLANES, :] = p_ref[tm:tm + SUBLANES, :]


def _proj_gated_conv(h, w_in_t, layer, conv_w, tm=1024, tc=256):
    s, d = h.shape
    tm = min(tm, s)

    def w_spec(row0):
        return pl.BlockSpec((pl.Element(1), pl.Element(tc), pl.Element(d)),
                            lambda c, i: (layer, pl.multiple_of(row0 + c * tc, SUBLANES), 0))

    nbytes = 2 * tm * d * 2 + 3 * 2 * tc * d * 4 + 3 * tc * d * 2 + 2 * tm * tc * 2 + 6 * tm * tc * 4
    return pl.pallas_call(
        functools.partial(_proj_conv_kernel, tm=tm),
        grid=(CONV_DIM // tc, s // tm),
        in_specs=[pl.BlockSpec((tm, d), lambda c, i: (i, 0)),
                  w_spec(0), w_spec(CONV_DIM), w_spec(2 * CONV_DIM),
                  pl.BlockSpec((CONV_K, tc), lambda c, i: (0, c))],
        out_specs=pl.BlockSpec((tm, tc), lambda c, i: (i, c)),
        out_shape=jax.ShapeDtypeStruct((s, CONV_DIM), BF16),
        scratch_shapes=[pltpu.VMEM((3, tc, d), BF16), pltpu.VMEM((tm + SUBLANES, tc), F32)],
        compiler_params=pltpu.CompilerParams(
            dimension_semantics=("parallel", "arbitrary"),
            vmem_limit_bytes=_vmem_limit(nbytes)),
        name="proj_conv",
    )(h, w_in_t, w_in_t, w_in_t, conv_w)


def _t5_bucket_np(dist):
    n = np.maximum(dist, 0)
    max_exact = N_BUCKETS // 2
    nf = np.maximum(n, 1).astype(np.float64)
    large = max_exact + (np.log(nf / max_exact) / math.log(MAX_DISTANCE / max_exact)
                         * (N_BUCKETS - max_exact)).astype(np.int32)
    large = np.minimum(large, N_BUCKETS - 1)
    return np.where(n < max_exact, n, large).astype(np.int32)


def _near_bias(rel_bias):
    sl = np.arange(Q_TILE)[:, None]
    tl = np.arange(Q_TILE)[None, :]
    dist = np.stack([tl - sl + Q_TILE, tl - sl])
    assert _t5_bucket_np(np.array([Q_TILE]))[0] == N_BUCKETS - 1
    bucket = _t5_bucket_np(dist)
    rel = (rel_bias - rel_bias[N_BUCKETS - 1][None, :]) * (HEAD_DIM ** 0.5)
    onehot = np.eye(N_BUCKETS, dtype=np.float32)[bucket.reshape(-1)]
    b = jnp.dot(onehot, rel, precision=lax.Precision.HIGHEST)
    b = b.reshape(2, Q_TILE, Q_TILE, N_KV_HEADS, GROUP)
    b = b.transpose(0, 3, 1, 4, 2)
    b = b.reshape(2, N_KV_HEADS, Q_TILE, GROUP * Q_TILE).astype(F32)
    return jnp.concatenate([jnp.zeros_like(b[:1]), b], axis=0)


def _dsa_kernel(q_ref, k_ref, iq_lo_ref, iq_hi_ref, vt_ref, ik2_ref, w_ref, bias_ref, o_ref,
                sc_ref, s_ref, acc_ref, m_ref, *, topk, max_iters, pos_bits):
    i = pl.program_id(0)
    q0 = i * Q_TILE
    kf = float(topk)
    per = ATT_KEYS // Q_TILE
    c_last = i // per
    n_cnt = c_last + 1
    n_idx = (q0 + Q_TILE + IDX_KEYS - 1) // IDX_KEYS
    qpos = q0 + lax.broadcasted_iota(jnp.int32, (1, Q_TILE), 1)

    def fold8(x, op):
        rows = x.shape[0]
        if rows > SUBLANES * SUBLANES:
            x = op(x.reshape(SUBLANES, rows // SUBLANES, x.shape[1]), axis=0)
        return op(x.reshape(x.shape[0] // SUBLANES, SUBLANES, x.shape[1]), axis=0)

    def col_reduce(x, op):
        return op(fold8(x, op), axis=0, keepdims=True)

    n_dots = IDX_ALL // (2 * LANES)
    half = n_dots // 2

    def head_pairs(d):
        ref, e = (iq_lo_ref, d) if d < half else (iq_hi_ref, d - half)
        return jnp.concatenate([ref[:, (2 * e) * LANES:(2 * e + 1) * LANES],
                                ref[:, (2 * e + 1) * LANES:(2 * e + 2) * LANES]], axis=0)

    rhs = [head_pairs(d) for d in range(n_dots)]

    def idx_body(j, carry):
        mx, mn = carry
        s0 = pl.multiple_of(j * IDX_KEYS, IDX_KEYS)
        lhs = jnp.concatenate([ik2_ref[pl.ds(s0, IDX_KEYS), 0:LANES],
                               ik2_ref[pl.ds(s0, IDX_KEYS), LANES:2 * LANES]], axis=0)
        acc = None
        for d in range(n_dots):
            r = lax.dot_general(lhs, rhs[d], NT_DIMS, preferred_element_type=F32)
            r = jnp.maximum(r, 0.0)
            t = r[:IDX_KEYS] * w_ref[2 * d:2 * d + 1, :] + r[IDX_KEYS:] * w_ref[2 * d + 1:2 * d + 2, :]
            acc = t if acc is None else acc + t
        acc = acc[:, :Q_TILE] + acc[:, Q_TILE:]
        kpos = s0 + lax.broadcasted_iota(jnp.int32, (IDX_KEYS, 1), 0)
        causal = kpos <= qpos
        lo_fill = jnp.where(causal, acc, NEG_INF)
        sc_ref[pl.ds(s0, IDX_KEYS), :] = lo_fill
        mx = jnp.maximum(mx, fold8(lo_fill, jnp.max))
        mn = jnp.minimum(mn, fold8(jnp.where(causal, acc, -NEG_INF), jnp.min))
        return mx, mn

    mx8, mn8 = lax.fori_loop(
        0, n_idx, idx_body,
        (jnp.full((SUBLANES, Q_TILE), NEG_INF, F32), jnp.full((SUBLANES, Q_TILE), -NEG_INF, F32)))
    hi0 = jnp.max(mx8, axis=0, keepdims=True)
    lo0 = jnp.min(mn8, axis=0, keepdims=True)

    def fill_body(j, carry):
        sc_ref[pl.ds(pl.multiple_of(j * IDX_KEYS, IDX_KEYS), IDX_KEYS), :] = jnp.full(
            (IDX_KEYS, Q_TILE), NEG_INF, F32)
        return carry

    lax.fori_loop(n_idx, n_cnt * (ATT_KEYS // IDX_KEYS), fill_body, 0)

    def key_pos(c):
        return c * ATT_KEYS + lax.broadcasted_iota(jnp.int32, (ATT_KEYS, 1), 0)

    def count_rows(pred):
        def body(c, cnt):
            blk = sc_ref[pl.ds(pl.multiple_of(c * ATT_KEYS, ATT_KEYS), ATT_KEYS), :]
            return cnt + fold8(pred(blk, key_pos(c)), jnp.sum)
        c8 = lax.fori_loop(0, n_cnt, body, jnp.zeros((SUBLANES, Q_TILE), F32))
        return jnp.sum(c8, axis=0, keepdims=True)

    def count_ge(tau):
        slabs = SUBLANES
        rows = ATT_KEYS // slabs

        def body(c, cnt):
            blk = sc_ref[pl.ds(pl.multiple_of(c * ATT_KEYS, ATT_KEYS), ATT_KEYS), :]
            for j in range(slabs):
                cnt = jnp.where(blk[j * rows:(j + 1) * rows] >= tau, cnt + 1.0, cnt)
            return cnt

        cnt = lax.fori_loop(0, n_cnt, body, jnp.zeros((rows, Q_TILE), F32))
        return jnp.sum(fold8(cnt, jnp.sum), axis=0, keepdims=True)

    def n_active(lo, hi, flo):
        mid = 0.5 * lo + 0.5 * hi
        act = jnp.logical_and(flo > kf, jnp.logical_and(mid > lo, mid < hi))
        return act, jnp.sum(jnp.where(act, 1.0, 0.0))

    flo0 = (qpos + 1).astype(F32)

    def bis_cond(st):
        _, _, _, n, it = st
        return jnp.logical_and(n > 0.0, it < max_iters)

    def bis_body(st):
        lo, hi, flo, _, it = st
        for _ in range(BISECT_STEPS):
            act, _ = n_active(lo, hi, flo)
            mid = 0.5 * lo + 0.5 * hi
            c = count_ge(mid)
            up = jnp.logical_and(act, c >= kf)
            dn = jnp.logical_and(act, c < kf)
            lo = jnp.where(up, mid, lo)
            flo = jnp.where(up, c, flo)
            hi = jnp.where(dn, mid, hi)
        _, n = n_active(lo, hi, flo)
        return lo, hi, flo, n, it + 1

    _, n0 = n_active(lo0, hi0, flo0)
    lo, hi, flo, _, _ = lax.while_loop(bis_cond, bis_body, (lo0, hi0, flo0, n0, jnp.int32(0)))
    open_hi = jnp.logical_and(flo > kf, hi == hi0)
    n_open = jnp.sum(jnp.where(open_hi, 1.0, 0.0))
    c_hi = lax.cond(n_open > 0.0, lambda: count_ge(hi), lambda: jnp.zeros((1, Q_TILE), F32))
    use_hi = jnp.logical_and(open_hi, c_hi >= kf)
    tau = jnp.where(use_hi, hi, lo)
    n_ge = jnp.where(use_hi, c_hi, flo)

    tied = n_ge > kf
    n_keys = n_cnt * ATT_KEYS

    def tie_cut():
        n_gt = count_rows(lambda blk, pos: jnp.where(blk > tau, 1.0, 0.0))
        need = kf - n_gt

        def step(_, st):
            lo_p, hi_p = st
            mid = lax.shift_right_logical(lo_p + hi_p, 1)
            below = count_rows(
                lambda blk, pos: jnp.where(blk == tau, jnp.where(pos < mid, 1.0, 0.0), 0.0))
            ge = below >= need
            return jnp.where(ge, lo_p, mid), jnp.where(ge, mid, hi_p)

        lo_p = jnp.zeros((1, Q_TILE), jnp.int32)
        hi_p = jnp.zeros((1, Q_TILE), jnp.int32) + n_keys
        _, hi_p = lax.fori_loop(0, pos_bits, step, (lo_p, hi_p))
        return hi_p

    n_tied = jnp.sum(jnp.where(tied, 1.0, 0.0))
    cut = lax.cond(n_tied > 0.0, tie_cut, lambda: jnp.zeros((1, Q_TILE), jnp.int32) + n_keys)
    cut = jnp.where(tied, cut, n_keys)

    def mask_body(c, carry):
        rows = pl.ds(pl.multiple_of(c * ATT_KEYS, ATT_KEYS), ATT_KEYS)
        blk = sc_ref[rows, :]
        at_tau = jnp.where(key_pos(c) < cut, 0.0, NEG_INF)
        sc_ref[rows, :] = jnp.where(blk > tau, 0.0, jnp.where(blk == tau, at_tau, NEG_INF))
        return carry

    lax.fori_loop(0, n_cnt, mask_body, 0)

    c2 = (HEAD_DIM ** -0.5) * math.log2(math.e)
    qg = []
    for g in range(N_KV_HEADS):
        qg.append(jnp.concatenate(
            [q_ref[:, (g * GROUP + hh) * HEAD_DIM:(g * GROUP + hh + 1) * HEAD_DIM] for hh in range(GROUP)],
            axis=0))

    m_ref[...] = jnp.full(m_ref.shape, NEG_INF, F32)
    acc_ref[...] = jnp.zeros(acc_ref.shape, F32)

    def logits(c, g, slot):
        s0 = pl.multiple_of(jnp.minimum(c, n_cnt - 1) * ATT_KEYS, ATT_KEYS)
        kb = k_ref[pl.ds(s0, ATT_KEYS), g * HEAD_DIM:(g + 1) * HEAD_DIM]
        s_ref[slot] = lax.dot_general(kb, qg[g], NT_DIMS, preferred_element_type=F32)

    def chunk(c, near):
        s0 = pl.multiple_of(c * ATT_KEYS, ATT_KEYS)
        madd = jnp.concatenate([sc_ref[pl.ds(s0, ATT_KEYS), :]] * GROUP, axis=1)
        for g in range(N_KV_HEADS):
            if g + 1 < N_KV_HEADS:
                logits(c, g + 1, (g + 1) % 2)
            else:
                logits(c + 1, 0, 0)
            vtb = vt_ref[g, :, pl.ds(s0, ATT_KEYS)]
            s = s_ref[g % 2] + madd
            if near:
                rows = []
                for r in range(per):
                    d = i - (c * per + r)
                    rows.append(bias_ref[jnp.where(d == 1, 1, jnp.where(d == 0, 2, 0)), g])
                s = s + jnp.concatenate(rows, axis=0)
            m_old = m_ref[g]
            m_new = jnp.maximum(m_old, col_reduce(s, jnp.max))
            m_use = jnp.where(m_new == NEG_INF, 0.0, m_new)
            alpha = jnp.exp2(c2 * (m_old - m_use))
            p = jnp.exp2(c2 * (s - m_use))
            pv = jnp.dot(vtb, p.astype(BF16), preferred_element_type=F32)
            acc_ref[g] = alpha * acc_ref[g] + pv
            m_ref[g] = m_new

    n_far = jnp.where(i % per == 0, jnp.maximum(c_last - 1, 0), c_last)
    logits(0, 0, 0)

    def far_body(c, carry):
        chunk(c, False)
        return carry

    lax.fori_loop(0, n_far, far_body, 0)

    def near_body(c, carry):
        chunk(c, True)
        return carry

    lax.fori_loop(n_far, n_cnt, near_body, 0)

    for g in range(N_KV_HEADS):
        out = acc_ref[g, 0:HEAD_DIM, :] / acc_ref[g, HEAD_DIM:HEAD_DIM + 1, :]
        for hh in range(GROUP):
            h = g * GROUP + hh
            o_ref[:, h * HEAD_DIM:(h + 1) * HEAD_DIM] = (
                out[:, hh * Q_TILE:(hh + 1) * Q_TILE].T.astype(o_ref.dtype))


def _dsa_attention(zb, zc, rel_bias, topk):
    s = zb.shape[0]
    nq = s // Q_TILE
    assert s % ATT_KEYS == 0
    k_col = ATTN_DIM // KV_DIM
    iq_half = IDX_ALL // 2
    iq_col = (ATTN_DIM + 2 * KV_DIM) // iq_half
    assert (ATTN_DIM + 2 * KV_DIM) % iq_half == 0
    vt = zb[:, ATTN_DIM + KV_DIM:ATTN_DIM + 2 * KV_DIM].T.reshape(N_KV_HEADS, HEAD_DIM, s)
    vt = jnp.concatenate([vt, jnp.ones((N_KV_HEADS, V_ROWS - HEAD_DIM, s), BF16)], axis=1)
    ik = zc[:, :IDX_DIM].astype(BF16)
    ik2 = jnp.concatenate([ik, jnp.zeros((s, 2 * IDX_DIM), BF16), ik], axis=1)
    w = zc[:, IDX_DIM:IDX_DIM + IDX_HEADS] * (IDX_HEADS ** -0.5) * (IDX_DIM ** -0.5)
    w = w.reshape(nq, Q_TILE, IDX_HEADS // 4, 2, 2).transpose(0, 2, 4, 3, 1)
    w = w.reshape(nq, IDX_HEADS // 2, 2 * Q_TILE)
    bias = _near_bias(rel_bias)

    resident = dict(pipeline_mode=pl.Buffered(1))
    nbytes = (s * KV_DIM * 2 + vt.size * 2 + s * 2 * LANES * 2 + bias.size * 4 + s * Q_TILE * 4
              + 2 * (2 * Q_TILE * ATTN_DIM * 2 * 2 + IDX_HEADS * Q_TILE * 4)
              + 6 * ATT_KEYS * GROUP * Q_TILE * 4 + 3 * N_KV_HEADS * V_ROWS * GROUP * Q_TILE * 4)
    return pl.pallas_call(
        functools.partial(_dsa_kernel, topk=topk, max_iters=400, pos_bits=int(s).bit_length()),
        grid=(nq,),
        in_specs=[pl.BlockSpec((Q_TILE, ATTN_DIM), lambda i: (i, 0)),
                  pl.BlockSpec((s, KV_DIM), lambda i: (0, k_col), **resident),
                  pl.BlockSpec((Q_TILE, iq_half), lambda i: (i, iq_col)),
                  pl.BlockSpec((Q_TILE, iq_half), lambda i: (i, iq_col + 1)),
                  pl.BlockSpec(vt.shape, lambda i: (0, 0, 0), **resident),
                  pl.BlockSpec((s, 2 * LANES), lambda i: (0, 0), **resident),
                  pl.BlockSpec((None, IDX_HEADS // 2, 2 * Q_TILE), lambda i: (i, 0, 0)),
                  pl.BlockSpec(bias.shape, lambda i: (0, 0, 0, 0), **resident)],
        out_specs=pl.BlockSpec((Q_TILE, ATTN_DIM), lambda i: (i, 0)),
        out_shape=jax.ShapeDtypeStruct((s, ATTN_DIM), BF16),
        scratch_shapes=[pltpu.VMEM((s, Q_TILE), F32),
                        pltpu.VMEM((2, ATT_KEYS, GROUP * Q_TILE), F32),
                        pltpu.VMEM((N_KV_HEADS, V_ROWS, GROUP * Q_TILE), F32),
                        pltpu.VMEM((N_KV_HEADS, 1, GROUP * Q_TILE), F32)],
        compiler_params=pltpu.CompilerParams(
            dimension_semantics=("arbitrary",),
            vmem_limit_bytes=_vmem_limit(nbytes)),
        name="dsa_attention",
    )(zb, zb, zb, zb, vt, ik2, w, bias)


def _merge_kernel(a_ref, o_ref, wa_ref, wo_ref, ga_ref, gb_ref, out_ref, wab_ref, wob_ref):
    @pl.when(pl.program_id(1) == 0)
    def _():
        wab_ref[...] = wa_ref[...].astype(BF16)
        wob_ref[...] = wo_ref[...].astype(BF16)

    ya = jnp.dot(a_ref[...], wab_ref[...], preferred_element_type=F32)
    yb = jnp.dot(o_ref[...], wob_ref[...], preferred_element_type=F32)
    merged = jax.nn.sigmoid(ga_ref[...]) * ya + jax.nn.sigmoid(gb_ref[...]) * yb
    out_ref[...] = merged.astype(out_ref.dtype)


def _merge(a_in, o, w_conv_out, w_attn_out, layer, zg, g_col0, tm=512, tn=1024):
    s = a_in.shape[0]
    tm = min(tm, s)
    nn = D_MODEL // tn
    assert g_col0 % tn == 0
    gj = g_col0 // tn
    once = dict(pipeline_mode=pl.Buffered(1))
    nbytes = (2 * (2 * tm * CONV_DIM * 2 + 2 * tm * tn * 4 + tm * tn * 2) + 2 * CONV_DIM * tn * 4
              + 2 * CONV_DIM * tn * 2 + 4 * tm * tn * 4)
    return pl.pallas_call(
        _merge_kernel,
        grid=(nn, s // tm),
        in_specs=[pl.BlockSpec((tm, CONV_DIM), lambda j, i: (i, 0)),
                  pl.BlockSpec((tm, ATTN_DIM), lambda j, i: (i, 0)),
                  pl.BlockSpec((None, CONV_DIM, tn), lambda j, i: (layer, 0, j), **once),
                  pl.BlockSpec((None, ATTN_DIM, tn), lambda j, i: (layer, 0, j), **once),
                  pl.BlockSpec((tm, tn), lambda j, i: (i, j + gj)),
                  pl.BlockSpec((tm, tn), lambda j, i: (i, j + gj + nn))],
        out_specs=pl.BlockSpec((tm, tn), lambda j, i: (i, j)),
        out_shape=jax.ShapeDtypeStruct((s, D_MODEL), BF16),
        scratch_shapes=[pltpu.VMEM((CONV_DIM, tn), BF16), pltpu.VMEM((ATTN_DIM, tn), BF16)],
        compiler_params=pltpu.CompilerParams(
            dimension_semantics=("parallel", "arbitrary"),
            vmem_limit_bytes=_vmem_limit(nbytes)),
        name="merge",
    )(a_in, o, w_conv_out, w_attn_out, zg, zg)


def _xattn_kernel(x_ref, gx_ref, gm_ref, wq_ref, kx_ref, vx_ref, wo_ref, out_ref, h_ref, wqb_ref, wob_ref):
    @pl.when(pl.program_id(0) == 0)
    def _():
        wqb_ref[...] = wq_ref[...].astype(BF16)
        wob_ref[...] = wo_ref[...].astype(BF16)

    def rms(v, g_ref):
        inv = lax.rsqrt(jnp.mean(v * v, axis=-1, keepdims=True) + EPS)
        return ((v * inv) * g_ref[...]).astype(BF16)

    x = x_ref[...]
    qx = jnp.dot(rms(x, gx_ref), wqb_ref[...], preferred_element_type=F32).astype(BF16)
    scale = X_HEAD_DIM ** -0.5
    outs = []
    for h in range(X_HEADS):
        sl = slice(h * X_HEAD_DIM, (h + 1) * X_HEAD_DIM)
        s = lax.dot_general(qx[:, sl], kx_ref[:, sl], NT_DIMS, preferred_element_type=F32) * scale
        m = jnp.max(s, axis=-1, keepdims=True)
        p = jnp.exp(s - m)
        l = jnp.sum(p, axis=-1, keepdims=True)
        oh = jnp.dot(p.astype(BF16), vx_ref[:, sl], preferred_element_type=F32) / l
        outs.append(oh.astype(BF16))
    o = jnp.concatenate(outs, axis=1)
    x_new = x + jnp.dot(o, wob_ref[...], preferred_element_type=F32)
    out_ref[...] = x_new
    h_ref[...] = rms(x_new, gm_ref)


def _cross_attention(x, g_xattn, g_mlp, w_xq, kv, w_xo, layer, tm=256):
    s = x.shape[0]
    tm = min(tm, s)
    n_mem = kv.shape[0]
    resident = dict(pipeline_mode=pl.Buffered(1))
    nbytes = (2 * (2 * tm * D_MODEL * 4 + tm * D_MODEL * 2) + 2 * D_MODEL * X_DIM * (4 + 2)
              + 2 * n_mem * X_DIM * 2 + 4 * tm * D_MODEL * 4)
    return pl.pallas_call(
        _xattn_kernel,
        grid=(s // tm,),
        in_specs=[pl.BlockSpec((tm, D_MODEL), lambda i: (i, 0)),
                  pl.BlockSpec((1, D_MODEL), lambda i: (0, 0)),
                  pl.BlockSpec((1, D_MODEL), lambda i: (0, 0)),
                  pl.BlockSpec((None, D_MODEL, X_DIM), lambda i: (layer, 0, 0), **resident),
                  pl.BlockSpec((n_mem, X_DIM), lambda i: (0, 0), **resident),
                  pl.BlockSpec((n_mem, X_DIM), lambda i: (0, 1), **resident),
                  pl.BlockSpec((None, X_DIM, D_MODEL), lambda i: (layer, 0, 0), **resident)],
        out_specs=[pl.BlockSpec((tm, D_MODEL), lambda i: (i, 0)),
                   pl.BlockSpec((tm, D_MODEL), lambda i: (i, 0))],
        out_shape=[jax.ShapeDtypeStruct((s, D_MODEL), F32),
                   jax.ShapeDtypeStruct((s, D_MODEL), BF16)],
        scratch_shapes=[pltpu.VMEM((D_MODEL, X_DIM), BF16), pltpu.VMEM((X_DIM, D_MODEL), BF16)],
        compiler_params=pltpu.CompilerParams(
            dimension_semantics=("arbitrary",),
            vmem_limit_bytes=_vmem_limit(nbytes)),
        name="cross_attention",
    )(x, g_xattn.reshape(1, D_MODEL), g_mlp.reshape(1, D_MODEL), w_xq, kv, kv, w_xo)


CONV_END = 3 * CONV_DIM
ATTN_END = CONV_END + ATTN_DIM + 2 * KV_DIM + IDX_ALL
IW_END = ATTN_END + IDX_DIM + IDX_HEADS


def kernel(x, mem, rel_bias, norm_mix, w_in, conv_w, w_conv_out, w_attn_out, w_mix_out, norm_xattn, norm_mem, w_xq, w_xkv, w_xo, norm_mlp, w_up, w_down, norm_final):
    bsz, s, d = x.shape
    assert bsz == 1 and d == D_MODEL
    depth = w_in.shape[0]
    topk = min(TOPK_MAX, s // 4)
    xs = x.reshape(s, d)
    mems = mem.reshape(mem.shape[1], d)
    w_in_t = jnp.swapaxes(w_in, 1, 2)
    dx = None
    for l in range(depth):
        if dx is None:
            h = _rmsnorm(xs, norm_mix[l], BF16)
        else:
            xs, h = _add_rmsnorm(xs, dx, norm_mix[l], BF16, keep_sum=True)
        proj = functools.partial(_matmul, h, w_in_t, layer=l, b_transposed=True, tm=2048)
        a_in = _proj_gated_conv(h, w_in_t, l, conv_w[l])
        zb = proj(col0=CONV_END, n=ATTN_END - CONV_END, out_dtype=BF16, tn=512, name="proj_attn")
        zc = _matmul(h, w_in_t, layer=l, b_transposed=True, col0=ATTN_END, n=LANES, out_dtype=F32,
                     tm=1024, tn=LANES, name="proj_idx")
        zg = proj(col0=IW_END, n=2 * D_MODEL, out_dtype=F32, tn=512, name="proj_gate")
        o = _dsa_attention(zb, zc, rel_bias, topk)
        merged = _merge(a_in, o, w_conv_out, w_attn_out, l, zg, 0)
        xs = _matmul(merged, w_mix_out, layer=l, out_dtype=F32, tm=1024, tn=512,
                     epilogue="residual", residual=xs, name="mix_out")
        hm = _rmsnorm(mems, norm_mem[l], BF16)
        kv = _matmul(hm, w_xkv, layer=l, out_dtype=BF16, tm=256, tn=512, name="mem_kv")
        xs, hmlp = _cross_attention(xs, norm_xattn[l], norm_mlp[l], w_xq, kv, w_xo, l)
        act = _matmul(hmlp, w_up, layer=l, out_dtype=BF16, tm=2048, tn=512, epilogue="relu2", name="mlp_up")
        dx = _matmul(act, w_down, layer=l, out_dtype=F32, tm=2048, tn=1024, tk=1024, name="mlp_down")
    out = _add_rmsnorm(xs, dx, norm_final, F32, keep_sum=False)
    return out.reshape(bsz, s, d)
```

```python
import functools
import math

import numpy as np
import jax
import jax.numpy as jnp
from jax import lax
from jax.experimental import pallas as pl
from jax.experimental.pallas import tpu as pltpu

F32 = jnp.float32
BF16 = jnp.bfloat16

D_MODEL = 4096
CONV_DIM = 2048
CONV_K = 3
N_HEADS = 16
N_KV_HEADS = 4
HEAD_DIM = 128
GROUP = N_HEADS // N_KV_HEADS
ATTN_DIM = N_HEADS * HEAD_DIM
KV_DIM = N_KV_HEADS * HEAD_DIM
IDX_HEADS = 32
IDX_DIM = 64
IDX_ALL = IDX_HEADS * IDX_DIM
TOPK_MAX = 256
N_BUCKETS = 32
MAX_DISTANCE = 128
X_HEADS = 4
X_HEAD_DIM = 128
X_DIM = X_HEADS * X_HEAD_DIM
EPS = 1e-6

V7X_VMEM_BYTES = 64 * 1024 * 1024
LANES = 128
SUBLANES = 8

Q_TILE = 128
IDX_KEYS = 512
ATT_KEYS = 512
V_ROWS = HEAD_DIM + 16
BISECT_STEPS = 4
NEG_INF = float("-inf")
NT_DIMS = (((1,), (1,)), ((), ()))


INTERNAL_SCRATCH_BYTES = 8 << 20
VMEM_RESERVED_BYTES = 6 << 20


def _vmem_limit(nbytes):
    return int(min(nbytes + INTERNAL_SCRATCH_BYTES, V7X_VMEM_BYTES - VMEM_RESERVED_BYTES))


def _rmsnorm_kernel(x_ref, g_ref, o_ref):
    x = x_ref[...]
    inv = lax.rsqrt(jnp.mean(x * x, axis=-1, keepdims=True) + EPS)
    o_ref[...] = ((x * inv) * g_ref[...]).astype(o_ref.dtype)


def _rmsnorm(x, g, out_dtype, tr=256):
    rows, d = x.shape
    tr = min(tr, rows)
    return pl.pallas_call(
        _rmsnorm_kernel,
        grid=(rows // tr,),
        in_specs=[pl.BlockSpec((tr, d), lambda i: (i, 0)),
                  pl.BlockSpec((1, d), lambda i: (0, 0))],
        out_specs=pl.BlockSpec((tr, d), lambda i: (i, 0)),
        out_shape=jax.ShapeDtypeStruct((rows, d), out_dtype),
        compiler_params=pltpu.CompilerParams(
            dimension_semantics=("parallel",),
            vmem_limit_bytes=_vmem_limit(2 * tr * d * (4 + jnp.dtype(out_dtype).itemsize))),
        name="rmsnorm",
    )(x, g.reshape(1, d))


def _add_rmsnorm_kernel(x_ref, dx_ref, g_ref, *out_refs):
    x = x_ref[...] + dx_ref[...]
    inv = lax.rsqrt(jnp.mean(x * x, axis=-1, keepdims=True) + EPS)
    h_ref = out_refs[-1]
    h_ref[...] = ((x * inv) * g_ref[...]).astype(h_ref.dtype)
    if len(out_refs) == 2:
        out_refs[0][...] = x


def _add_rmsnorm(x, dx, g, out_dtype, keep_sum, tr=256):
    rows, d = x.shape
    tr = min(tr, rows)
    row_spec = pl.BlockSpec((tr, d), lambda i: (i, 0))
    out_specs = [row_spec, row_spec] if keep_sum else [row_spec]
    out_shape = [jax.ShapeDtypeStruct((rows, d), out_dtype)]
    if keep_sum:
        out_shape.insert(0, jax.ShapeDtypeStruct((rows, d), F32))
    outs = pl.pallas_call(
        _add_rmsnorm_kernel,
        grid=(rows // tr,),
        in_specs=[row_spec, row_spec, pl.BlockSpec((1, d), lambda i: (0, 0))],
        out_specs=out_specs,
        out_shape=out_shape,
        compiler_params=pltpu.CompilerParams(
            dimension_semantics=("parallel",),
            vmem_limit_bytes=_vmem_limit(2 * tr * d * (12 + jnp.dtype(out_dtype).itemsize))),
        name="add_rmsnorm",
    )(x, dx, g.reshape(1, d))
    return outs if keep_sum else outs[0]


def _mm_kernel(*refs, nk, epilogue, b_transposed):
    if epilogue == "residual":
        a_ref, b_ref, r_ref, o_ref = refs
    else:
        a_ref, b_ref, o_ref = refs
        r_ref = None

    def product():
        if b_transposed:
            return lax.dot_general(a_ref[...], b_ref[0].astype(BF16), NT_DIMS, preferred_element_type=F32)
        return jnp.dot(a_ref[...], b_ref[...].astype(BF16), preferred_element_type=F32)

    if nk > 1:
        @pl.when(pl.program_id(2) == 0)
        def _():
            o_ref[...] = jnp.zeros(o_ref.shape, o_ref.dtype)

        o_ref[...] += product()
        return
    acc = product()
    if epilogue == "relu2":
        acc = jnp.square(jnp.maximum(acc, 0.0))
    elif epilogue == "residual":
        acc = r_ref[...] + acc
    o_ref[...] = acc.astype(o_ref.dtype)


def _matmul(a, b, *, out_dtype, tm, tn, tk=None, layer=None, col0=0, n=None, b_transposed=False,
            epilogue=None, residual=None, name="matmul"):
    m, kdim = a.shape
    n_all = b.shape[-2] if b_transposed else b.shape[-1]
    n = (n_all - col0) if n is None else n
    tm, tn = min(tm, m), min(tn, n)
    tk = kdim if tk is None else tk
    nk = kdim // tk
    assert m % tm == 0 and n % tn == 0 and kdim % tk == 0
    if b_transposed:
        assert layer is not None and col0 % SUBLANES == 0
        b_spec = pl.BlockSpec((pl.Element(1), pl.Element(tn), pl.Element(tk)),
                              lambda i, j, k: (layer, pl.multiple_of(col0 + j * tn, SUBLANES), k * tk))
    else:
        assert col0 % tn == 0
        j0 = col0 // tn
        if layer is None:
            b_spec = pl.BlockSpec((tk, tn), lambda i, j, k: (k, j + j0))
        else:
            b_spec = pl.BlockSpec((None, tk, tn), lambda i, j, k: (layer, k, j + j0))
    a_buffers = 1 if (nk == 1 and tm * tk * 2 > V7X_VMEM_BYTES // 8) else 2
    in_specs = [pl.BlockSpec((tm, tk), lambda i, j, k: (i, k), pipeline_mode=pl.Buffered(a_buffers)), b_spec]
    args = [a, b]
    b_item = jnp.dtype(b.dtype).itemsize
    nbytes = a_buffers * tm * tk * 2 + 2 * tk * tn * b_item + 2 * tm * tn * jnp.dtype(out_dtype).itemsize
    if b_item != 2:
        nbytes += tk * tn * 2
    if epilogue == "residual":
        in_specs.append(pl.BlockSpec((tm, tn), lambda i, j, k: (i, j)))
        args.append(residual)
        nbytes += 2 * tm * tn * 4
    assert nk == 1 or (epilogue is None and out_dtype == F32)
    nbytes += tm * tn * 4
    return pl.pallas_call(
        functools.partial(_mm_kernel, nk=nk, epilogue=epilogue, b_transposed=b_transposed),
        grid=(m // tm, n // tn, nk),
        in_specs=in_specs,
        out_specs=pl.BlockSpec((tm, tn), lambda i, j, k: (i, j)),
        out_shape=jax.ShapeDtypeStruct((m, n), out_dtype),
        compiler_params=pltpu.CompilerParams(
            dimension_semantics=("parallel", "parallel", "arbitrary"),
            vmem_limit_bytes=_vmem_limit(nbytes)),
        name=name,
    )(*args)


def _proj_conv_kernel(h_ref, wu_ref, wcb_ref, wcc_ref, cw_ref, o_ref, wb_ref, p_ref, *, tm):
    @pl.when(pl.program_id(1) == 0)
    def _():
        wb_ref[0] = wu_ref[0].astype(BF16)
        wb_ref[1] = wcb_ref[0].astype(BF16)
        wb_ref[2] = wcc_ref[0].astype(BF16)
        p_ref[0:SUBLANES, :] = jnp.zeros((SUBLANES, p_ref.shape[1]), F32)

    h = h_ref[...]
    u = lax.dot_general(h, wb_ref[0], NT_DIMS, preferred_element_type=F32)
    cb = lax.dot_general(h, wb_ref[1], NT_DIMS, preferred_element_type=F32)
    cc = lax.dot_general(h, wb_ref[2], NT_DIMS, preferred_element_type=F32)
    p = cc * u
    p_ref[SUBLANES:SUBLANES + tm, :] = p
    p1 = p_ref[SUBLANES - 1:SUBLANES - 1 + tm, :]
    p2 = p_ref[SUBLANES - 2:SUBLANES - 2 + tm, :]
    w = cw_ref[...]
    z = w[2:3, :] * p + w[0:1, :] * p2 + w[1:2, :] * p1
    o_ref[...] = (cb * z).astype(o_ref.dtype)
    p_ref[0:SUBLANES, :] = p_ref[tm:tm + SUBLANES, :]


def _proj_gated_conv(h, w_in_t, layer, conv_w, tm=1024, tc=256):
    s, d = h.shape
    tm = min(tm, s)

    def w_spec(row0):
        return pl.BlockSpec((pl.Element(1), pl.Element(tc), pl.Element(d)),
                            lambda c, i: (layer, pl.multiple_of(row0 + c * tc, SUBLANES), 0))

    nbytes = 2 * tm * d * 2 + 3 * 2 * tc * d * 4 + 3 * tc * d * 2 + 2 * tm * tc * 2 + 6 * tm * tc * 4
    return pl.pallas_call(
        functools.partial(_proj_conv_kernel, tm=tm),
        grid=(CONV_DIM // tc, s // tm),
        in_specs=[pl.BlockSpec((tm, d), lambda c, i: (i, 0)),
                  w_spec(0), w_spec(CONV_DIM), w_spec(2 * CONV_DIM),
                  pl.BlockSpec((CONV_K, tc), lambda c, i: (0, c))],
        out_specs=pl.BlockSpec((tm, tc), lambda c, i: (i, c)),
        out_shape=jax.ShapeDtypeStruct((s, CONV_DIM), BF16),
        scratch_shapes=[pltpu.VMEM((3, tc, d), BF16), pltpu.VMEM((tm + SUBLANES, tc), F32)],
        compiler_params=pltpu.CompilerParams(
            dimension_semantics=("parallel", "arbitrary"),
            vmem_limit_bytes=_vmem_limit(nbytes)),
        name="proj_conv",
    )(h, w_in_t, w_in_t, w_in_t, conv_w)


def _t5_bucket_np(dist):
    n = np.maximum(dist, 0)
    max_exact = N_BUCKETS // 2
    nf = np.maximum(n, 1).astype(np.float64)
    large = max_exact + (np.log(nf / max_exact) / math.log(MAX_DISTANCE / max_exact)
                         * (N_BUCKETS - max_exact)).astype(np.int32)
    large = np.minimum(large, N_BUCKETS - 1)
    return np.where(n < max_exact, n, large).astype(np.int32)


def _near_bias(rel_bias):
    sl = np.arange(Q_TILE)[:, None]
    tl = np.arange(Q_TILE)[None, :]
    dist = np.stack([tl - sl + Q_TILE, tl - sl])
    assert _t5_bucket_np(np.array([Q_TILE]))[0] == N_BUCKETS - 1
    bucket = _t5_bucket_np(dist)
    rel = (rel_bias - rel_bias[N_BUCKETS - 1][None, :]) * (HEAD_DIM ** 0.5)
    onehot = np.eye(N_BUCKETS, dtype=np.float32)[bucket.reshape(-1)]
    b = jnp.dot(onehot, rel, precision=lax.Precision.HIGHEST)
    b = b.reshape(2, Q_TILE, Q_TILE, N_KV_HEADS, GROUP)
    b = b.transpose(0, 3, 1, 4, 2)
    b = b.reshape(2, N_KV_HEADS, Q_TILE, GROUP * Q_TILE).astype(F32)
    return jnp.concatenate([jnp.zeros_like(b[:1]), b], axis=0)


def _dsa_kernel(q_ref, k_ref, iq_lo_ref, iq_hi_ref, vt_ref, ik2_ref, w_ref, bias_ref, o_ref,
                sc_ref, s_ref, acc_ref, m_ref, *, topk, max_iters, pos_bits):
    i = pl.program_id(0)
    q0 = i * Q_TILE
    kf = float(topk)
    per = ATT_KEYS // Q_TILE
    c_last = i // per
    n_cnt = c_last + 1
    n_idx = (q0 + Q_TILE + IDX_KEYS - 1) // IDX_KEYS
    qpos = q0 + lax.broadcasted_iota(jnp.int32, (1, Q_TILE), 1)

    def fold8(x, op):
        rows = x.shape[0]
        if rows > SUBLANES * SUBLANES:
            x = op(x.reshape(SUBLANES, rows // SUBLANES, x.shape[1]), axis=0)
        return op(x.reshape(x.shape[0] // SUBLANES, SUBLANES, x.shape[1]), axis=0)

    def col_reduce(x, op):
        return op(fold8(x, op), axis=0, keepdims=True)

    n_dots = IDX_ALL // (2 * LANES)
    half = n_dots // 2

    def head_pairs(d):
        ref, e = (iq_lo_ref, d) if d < half else (iq_hi_ref, d - half)
        return jnp.concatenate([ref[:, (2 * e) * LANES:(2 * e + 1) * LANES],
                                ref[:, (2 * e + 1) * LANES:(2 * e + 2) * LANES]], axis=0)

    rhs = [head_pairs(d) for d in range(n_dots)]

    def idx_body(j, carry):
        mx, mn = carry
        s0 = pl.multiple_of(j * IDX_KEYS, IDX_KEYS)
        lhs = jnp.concatenate([ik2_ref[pl.ds(s0, IDX_KEYS), 0:LANES],
                               ik2_ref[pl.ds(s0, IDX_KEYS), LANES:2 * LANES]], axis=0)
        acc = None
        for d in range(n_dots):
            r = lax.dot_general(lhs, rhs[d], NT_DIMS, preferred_element_type=F32)
            r = jnp.maximum(r, 0.0)
            t = r[:IDX_KEYS] * w_ref[2 * d:2 * d + 1, :] + r[IDX_KEYS:] * w_ref[2 * d + 1:2 * d + 2, :]
            acc = t if acc is None else acc + t
        acc = acc[:, :Q_TILE] + acc[:, Q_TILE:]
        kpos = s0 + lax.broadcasted_iota(jnp.int32, (IDX_KEYS, 1), 0)
        causal = kpos <= qpos
        lo_fill = jnp.where(causal, acc, NEG_INF)
        sc_ref[pl.ds(s0, IDX_KEYS), :] = lo_fill
        mx = jnp.maximum(mx, fold8(lo_fill, jnp.max))
        mn = jnp.minimum(mn, fold8(jnp.where(causal, acc, -NEG_INF), jnp.min))
        return mx, mn

    mx8, mn8 = lax.fori_loop(
        0, n_idx, idx_body,
        (jnp.full((SUBLANES, Q_TILE), NEG_INF, F32), jnp.full((SUBLANES, Q_TILE), -NEG_INF, F32)))
    hi0 = jnp.max(mx8, axis=0, keepdims=True)
    lo0 = jnp.min(mn8, axis=0, keepdims=True)

    def fill_body(j, carry):
        sc_ref[pl.ds(pl.multiple_of(j * IDX_KEYS, IDX_KEYS), IDX_KEYS), :] = jnp.full(
            (IDX_KEYS, Q_TILE), NEG_INF, F32)
        return carry

    lax.fori_loop(n_idx, n_cnt * (ATT_KEYS // IDX_KEYS), fill_body, 0)

    def key_pos(c):
        return c * ATT_KEYS + lax.broadcasted_iota(jnp.int32, (ATT_KEYS, 1), 0)

    def count_rows(pred):
        def body(c, cnt):
            blk = sc_ref[pl.ds(pl.multiple_of(c * ATT_KEYS, ATT_KEYS), ATT_KEYS), :]
            return cnt + fold8(pred(blk, key_pos(c)), jnp.sum)
        c8 = lax.fori_loop(0, n_cnt, body, jnp.zeros((SUBLANES, Q_TILE), F32))
        return jnp.sum(c8, axis=0, keepdims=True)

    def count_ge(tau):
        slabs = SUBLANES
        rows = ATT_KEYS // slabs

        def body(c, cnt):
            blk = sc_ref[pl.ds(pl.multiple_of(c * ATT_KEYS, ATT_KEYS), ATT_KEYS), :]
            for j in range(slabs):
                cnt = jnp.where(blk[j * rows:(j + 1) * rows] >= tau, cnt + 1.0, cnt)
            return cnt

        cnt = lax.fori_loop(0, n_cnt, body, jnp.zeros((rows, Q_TILE), F32))
        return jnp.sum(fold8(cnt, jnp.sum), axis=0, keepdims=True)

    def n_active(lo, hi, flo):
        mid = 0.5 * lo + 0.5 * hi
        act = jnp.logical_and(flo > kf, jnp.logical_and(mid > lo, mid < hi))
        return act, jnp.sum(jnp.where(act, 1.0, 0.0))

    flo0 = (qpos + 1).astype(F32)

    def bis_cond(st):
        _, _, _, n, it = st
        return jnp.logical_and(n > 0.0, it < max_iters)

    def bis_body(st):
        lo, hi, flo, _, it = st
        for _ in range(BISECT_STEPS):
            act, _ = n_active(lo, hi, flo)
            mid = 0.5 * lo + 0.5 * hi
            c = count_ge(mid)
            up = jnp.logical_and(act, c >= kf)
            dn = jnp.logical_and(act, c < kf)
            lo = jnp.where(up, mid, lo)
            flo = jnp.where(up, c, flo)
            hi = jnp.where(dn, mid, hi)
        _, n = n_active(lo, hi, flo)
        return lo, hi, flo, n, it + 1

    _, n0 = n_active(lo0, hi0, flo0)
    lo, hi, flo, _, _ = lax.while_loop(bis_cond, bis_body, (lo0, hi0, flo0, n0, jnp.int32(0)))
    open_hi = jnp.logical_and(flo > kf, hi == hi0)
    n_open = jnp.sum(jnp.where(open_hi, 1.0, 0.0))
    c_hi = lax.cond(n_open > 0.0, lambda: count_ge(hi), lambda: jnp.zeros((1, Q_TILE), F32))
    use_hi = jnp.logical_and(open_hi, c_hi >= kf)
    tau = jnp.where(use_hi, hi, lo)
    n_ge = jnp.where(use_hi, c_hi, flo)

    tied = n_ge > kf
    n_keys = n_cnt * ATT_KEYS

    def tie_cut():
        n_gt = count_rows(lambda blk, pos: jnp.where(blk > tau, 1.0, 0.0))
        need = kf - n_gt

        def step(_, st):
            lo_p, hi_p = st
            mid = lax.shift_right_logical(lo_p + hi_p, 1)
            below = count_rows(
                lambda blk, pos: jnp.where(blk == tau, jnp.where(pos < mid, 1.0, 0.0), 0.0))
            ge = below >= need
            return jnp.where(ge, lo_p, mid), jnp.where(ge, mid, hi_p)

        lo_p = jnp.zeros((1, Q_TILE), jnp.int32)
        hi_p = jnp.zeros((1, Q_TILE), jnp.int32) + n_keys
        _, hi_p = lax.fori_loop(0, pos_bits, step, (lo_p, hi_p))
        return hi_p

    n_tied = jnp.sum(jnp.where(tied, 1.0, 0.0))
    cut = lax.cond(n_tied > 0.0, tie_cut, lambda: jnp.zeros((1, Q_TILE), jnp.int32) + n_keys)
    cut = jnp.where(tied, cut, n_keys)

    def mask_body(c, carry):
        rows = pl.ds(pl.multiple_of(c * ATT_KEYS, ATT_KEYS), ATT_KEYS)
        blk = sc_ref[rows, :]
        at_tau = jnp.where(key_pos(c) < cut, 0.0, NEG_INF)
        sc_ref[rows, :] = jnp.where(blk > tau, 0.0, jnp.where(blk == tau, at_tau, NEG_INF))
        return carry

    lax.fori_loop(0, n_cnt, mask_body, 0)

    c2 = (HEAD_DIM ** -0.5) * math.log2(math.e)
    qg = []
    for g in range(N_KV_HEADS):
        qg.append(jnp.concatenate(
            [q_ref[:, (g * GROUP + hh) * HEAD_DIM:(g * GROUP + hh + 1) * HEAD_DIM] for hh in range(GROUP)],
            axis=0))

    m_ref[...] = jnp.full(m_ref.shape, NEG_INF, F32)
    acc_ref[...] = jnp.zeros(acc_ref.shape, F32)

    def logits(c, g, slot):
        s0 = pl.multiple_of(jnp.minimum(c, n_cnt - 1) * ATT_KEYS, ATT_KEYS)
        kb = k_ref[pl.ds(s0, ATT_KEYS), g * HEAD_DIM:(g + 1) * HEAD_DIM]
        s_ref[slot] = lax.dot_general(kb, qg[g], NT_DIMS, preferred_element_type=F32)

    def chunk(c, near):
        s0 = pl.multiple_of(c * ATT_KEYS, ATT_KEYS)
        madd = jnp.concatenate([sc_ref[pl.ds(s0, ATT_KEYS), :]] * GROUP, axis=1)
        for g in range(N_KV_HEADS):
            if g + 1 < N_KV_HEADS:
                logits(c, g + 1, (g + 1) % 2)
            else:
                logits(c + 1, 0, 0)
            vtb = vt_ref[g, :, pl.ds(s0, ATT_KEYS)]
            s = s_ref[g % 2] + madd
            if near:
                rows = []
                for r in range(per):
                    d = i - (c * per + r)
                    rows.append(bias_ref[jnp.where(d == 1, 1, jnp.where(d == 0, 2, 0)), g])
                s = s + jnp.concatenate(rows, axis=0)
            m_old = m_ref[g]
            m_new = jnp.maximum(m_old, col_reduce(s, jnp.max))
            m_use = jnp.where(m_new == NEG_INF, 0.0, m_new)
            alpha = jnp.exp2(c2 * (m_old - m_use))
            p = jnp.exp2(c2 * (s - m_use))
            pv = jnp.dot(vtb, p.astype(BF16), preferred_element_type=F32)
            acc_ref[g] = alpha * acc_ref[g] + pv
            m_ref[g] = m_new

    n_far = jnp.where(i % per == 0, jnp.maximum(c_last - 1, 0), c_last)
    logits(0, 0, 0)

    def far_body(c, carry):
        chunk(c, False)
        return carry

    lax.fori_loop(0, n_far, far_body, 0)

    def near_body(c, carry):
        chunk(c, True)
        return carry

    lax.fori_loop(n_far, n_cnt, near_body, 0)

    for g in range(N_KV_HEADS):
        out = acc_ref[g, 0:HEAD_DIM, :] / acc_ref[g, HEAD_DIM:HEAD_DIM + 1, :]
        for hh in range(GROUP):
            h = g * GROUP + hh
            o_ref[:, h * HEAD_DIM:(h + 1) * HEAD_DIM] = (
                out[:, hh * Q_TILE:(hh + 1) * Q_TILE].T.astype(o_ref.dtype))


def _dsa_attention(zb, zc, rel_bias, topk):
    s = zb.shape[0]
    nq = s // Q_TILE
    assert s % ATT_KEYS == 0
    k_col = ATTN_DIM // KV_DIM
    iq_half = IDX_ALL // 2
    iq_col = (ATTN_DIM + 2 * KV_DIM) // iq_half
    assert (ATTN_DIM + 2 * KV_DIM) % iq_half == 0
    vt = zb[:, ATTN_DIM + KV_DIM:ATTN_DIM + 2 * KV_DIM].T.reshape(N_KV_HEADS, HEAD_DIM, s)
    vt = jnp.concatenate([vt, jnp.ones((N_KV_HEADS, V_ROWS - HEAD_DIM, s), BF16)], axis=1)
    ik = zc[:, :IDX_DIM].astype(BF16)
    ik2 = jnp.concatenate([ik, jnp.zeros((s, 2 * IDX_DIM), BF16), ik], axis=1)
    w = zc[:, IDX_DIM:IDX_DIM + IDX_HEADS] * (IDX_HEADS ** -0.5) * (IDX_DIM ** -0.5)
    w = w.reshape(nq, Q_TILE, IDX_HEADS // 4, 2, 2).transpose(0, 2, 4, 3, 1)
    w = w.reshape(nq, IDX_HEADS // 2, 2 * Q_TILE)
    bias = _near_bias(rel_bias)

    resident = dict(pipeline_mode=pl.Buffered(1))
    nbytes = (s * KV_DIM * 2 + vt.size * 2 + s * 2 * LANES * 2 + bias.size * 4 + s * Q_TILE * 4
              + 2 * (2 * Q_TILE * ATTN_DIM * 2 * 2 + IDX_HEADS * Q_TILE * 4)
              + 6 * ATT_KEYS * GROUP * Q_TILE * 4 + 3 * N_KV_HEADS * V_ROWS * GROUP * Q_TILE * 4)
    return pl.pallas_call(
        functools.partial(_dsa_kernel, topk=topk, max_iters=400, pos_bits=int(s).bit_length()),
        grid=(nq,),
        in_specs=[pl.BlockSpec((Q_TILE, ATTN_DIM), lambda i: (i, 0)),
                  pl.BlockSpec((s, KV_DIM), lambda i: (0, k_col), **resident),
                  pl.BlockSpec((Q_TILE, iq_half), lambda i: (i, iq_col)),
                  pl.BlockSpec((Q_TILE, iq_half), lambda i: (i, iq_col + 1)),
                  pl.BlockSpec(vt.shape, lambda i: (0, 0, 0), **resident),
                  pl.BlockSpec((s, 2 * LANES), lambda i: (0, 0), **resident),
                  pl.BlockSpec((None, IDX_HEADS // 2, 2 * Q_TILE), lambda i: (i, 0, 0)),
                  pl.BlockSpec(bias.shape, lambda i: (0, 0, 0, 0), **resident)],
        out_specs=pl.BlockSpec((Q_TILE, ATTN_DIM), lambda i: (i, 0)),
        out_shape=jax.ShapeDtypeStruct((s, ATTN_DIM), BF16),
        scratch_shapes=[pltpu.VMEM((s, Q_TILE), F32),
                        pltpu.VMEM((2, ATT_KEYS, GROUP * Q_TILE), F32),
                        pltpu.VMEM((N_KV_HEADS, V_ROWS, GROUP * Q_TILE), F32),
                        pltpu.VMEM((N_KV_HEADS, 1, GROUP * Q_TILE), F32)],
        compiler_params=pltpu.CompilerParams(
            dimension_semantics=("arbitrary",),
            vmem_limit_bytes=_vmem_limit(nbytes)),
        name="dsa_attention",
    )(zb, zb, zb, zb, vt, ik2, w, bias)


def _merge_kernel(a_ref, o_ref, wa_ref, wo_ref, ga_ref, gb_ref, out_ref, wab_ref, wob_ref):
    @pl.when(pl.program_id(1) == 0)
    def _():
        wab_ref[...] = wa_ref[...].astype(BF16)
        wob_ref[...] = wo_ref[...].astype(BF16)

    ya = jnp.dot(a_ref[...], wab_ref[...], preferred_element_type=F32)
    yb = jnp.dot(o_ref[...], wob_ref[...], preferred_element_type=F32)
    merged = jax.nn.sigmoid(ga_ref[...]) * ya + jax.nn.sigmoid(gb_ref[...]) * yb
    out_ref[...] = merged.astype(out_ref.dtype)


def _merge(a_in, o, w_conv_out, w_attn_out, layer, zg, g_col0, tm=512, tn=1024):
    s = a_in.shape[0]
    tm = min(tm, s)
    nn = D_MODEL // tn
    assert g_col0 % tn == 0
    gj = g_col0 // tn
    once = dict(pipeline_mode=pl.Buffered(1))
    nbytes = (2 * (2 * tm * CONV_DIM * 2 + 2 * tm * tn * 4 + tm * tn * 2) + 2 * CONV_DIM * tn * 4
              + 2 * CONV_DIM * tn * 2 + 4 * tm * tn * 4)
    return pl.pallas_call(
        _merge_kernel,
        grid=(nn, s // tm),
        in_specs=[pl.BlockSpec((tm, CONV_DIM), lambda j, i: (i, 0)),
                  pl.BlockSpec((tm, ATTN_DIM), lambda j, i: (i, 0)),
                  pl.BlockSpec((None, CONV_DIM, tn), lambda j, i: (layer, 0, j), **once),
                  pl.BlockSpec((None, ATTN_DIM, tn), lambda j, i: (layer, 0, j), **once),
                  pl.BlockSpec((tm, tn), lambda j, i: (i, j + gj)),
                  pl.BlockSpec((tm, tn), lambda j, i: (i, j + gj + nn))],
        out_specs=pl.BlockSpec((tm, tn), lambda j, i: (i, j)),
        out_shape=jax.ShapeDtypeStruct((s, D_MODEL), BF16),
        scratch_shapes=[pltpu.VMEM((CONV_DIM, tn), BF16), pltpu.VMEM((ATTN_DIM, tn), BF16)],
        compiler_params=pltpu.CompilerParams(
            dimension_semantics=("parallel", "arbitrary"),
            vmem_limit_bytes=_vmem_limit(nbytes)),
        name="merge",
    )(a_in, o, w_conv_out, w_attn_out, zg, zg)


def _xattn_kernel(x_ref, gx_ref, gm_ref, wq_ref, kx_ref, vx_ref, wo_ref, out_ref, h_ref, wqb_ref, wob_ref):
    @pl.when(pl.program_id(0) == 0)
    def _():
        wqb_ref[...] = wq_ref[...].astype(BF16)
        wob_ref[...] = wo_ref[...].astype(BF16)

    def rms(v, g_ref):
        inv = lax.rsqrt(jnp.mean(v * v, axis=-1, keepdims=True) + EPS)
        return ((v * inv) * g_ref[...]).astype(BF16)

    x = x_ref[...]
    qx = jnp.dot(rms(x, gx_ref), wqb_ref[...], preferred_element_type=F32).astype(BF16)
    scale = X_HEAD_DIM ** -0.5
    outs = []
    for h in range(X_HEADS):
        sl = slice(h * X_HEAD_DIM, (h + 1) * X_HEAD_DIM)
        s = lax.dot_general(qx[:, sl], kx_ref[:, sl], NT_DIMS, preferred_element_type=F32) * scale
        m = jnp.max(s, axis=-1, keepdims=True)
        p = jnp.exp(s - m)
        l = jnp.sum(p, axis=-1, keepdims=True)
        oh = jnp.dot(p.astype(BF16), vx_ref[:, sl], preferred_element_type=F32) / l
        outs.append(oh.astype(BF16))
    o = jnp.concatenate(outs, axis=1)
    x_new = x + jnp.dot(o, wob_ref[...], preferred_element_type=F32)
    out_ref[...] = x_new
    h_ref[...] = rms(x_new, gm_ref)


def _cross_attention(x, g_xattn, g_mlp, w_xq, kv, w_xo, layer, tm=256):
    s = x.shape[0]
    tm = min(tm, s)
    n_mem = kv.shape[0]
    resident = dict(pipeline_mode=pl.Buffered(1))
    nbytes = (2 * (2 * tm * D_MODEL * 4 + tm * D_MODEL * 2) + 2 * D_MODEL * X_DIM * (4 + 2)
              + 2 * n_mem * X_DIM * 2 + 4 * tm * D_MODEL * 4)
    return pl.pallas_call(
        _xattn_kernel,
        grid=(s // tm,),
        in_specs=[pl.BlockSpec((tm, D_MODEL), lambda i: (i, 0)),
                  pl.BlockSpec((1, D_MODEL), lambda i: (0, 0)),
                  pl.BlockSpec((1, D_MODEL), lambda i: (0, 0)),
                  pl.BlockSpec((None, D_MODEL, X_DIM), lambda i: (layer, 0, 0), **resident),
                  pl.BlockSpec((n_mem, X_DIM), lambda i: (0, 0), **resident),
                  pl.BlockSpec((n_mem, X_DIM), lambda i: (0, 1), **resident),
                  pl.BlockSpec((None, X_DIM, D_MODEL), lambda i: (layer, 0, 0), **resident)],
        out_specs=[pl.BlockSpec((tm, D_MODEL), lambda i: (i, 0)),
                   pl.BlockSpec((tm, D_MODEL), lambda i: (i, 0))],
        out_shape=[jax.ShapeDtypeStruct((s, D_MODEL), F32),
                   jax.ShapeDtypeStruct((s, D_MODEL), BF16)],
        scratch_shapes=[pltpu.VMEM((D_MODEL, X_DIM), BF16), pltpu.VMEM((X_DIM, D_MODEL), BF16)],
        compiler_params=pltpu.CompilerParams(
            dimension_semantics=("arbitrary",),
            vmem_limit_bytes=_vmem_limit(nbytes)),
        name="cross_attention",
    )(x, g_xattn.reshape(1, D_MODEL), g_mlp.reshape(1, D_MODEL), w_xq, kv, kv, w_xo)


CONV_END = 3 * CONV_DIM
ATTN_END = CONV_END + ATTN_DIM + 2 * KV_DIM + IDX_ALL
IW_END = ATTN_END + IDX_DIM + IDX_HEADS


def kernel(x, mem, rel_bias, norm_mix, w_in, conv_w, w_conv_out, w_attn_out, w_mix_out, norm_xattn, norm_mem, w_xq, w_xkv, w_xo, norm_mlp, w_up, w_down, norm_final):
    bsz, s, d = x.shape
    assert bsz == 1 and d == D_MODEL
    depth = w_in.shape[0]
    topk = min(TOPK_MAX, s // 4)
    xs = x.reshape(s, d)
    mems = mem.reshape(mem.shape[1], d)
    w_in_t = jnp.swapaxes(w_in, 1, 2)
    dx = None
    for l in range(depth):
        if dx is None:
            h = _rmsnorm(xs, norm_mix[l], BF16)
        else:
            xs, h = _add_rmsnorm(xs, dx, norm_mix[l], BF16, keep_sum=True)
        proj = functools.partial(_matmul, h, w_in_t, layer=l, b_transposed=True, tm=2048)
        a_in = _proj_gated_conv(h, w_in_t, l, conv_w[l])
        zb = proj(col0=CONV_END, n=ATTN_END - CONV_END, out_dtype=BF16, tn=512, name="proj_attn")
        zc = _matmul(h, w_in_t, layer=l, b_transposed=True, col0=ATTN_END, n=LANES, out_dtype=F32,
                     tm=1024, tn=LANES, name="proj_idx")
        zg = proj(col0=IW_END, n=2 * D_MODEL, out_dtype=F32, tn=512, name="proj_gate")
        o = _dsa_attention(zb, zc, rel_bias, topk)
        merged = _merge(a_in, o, w_conv_out, w_attn_out, l, zg, 0)
        xs = _matmul(merged, w_mix_out, layer=l, out_dtype=F32, tm=1024, tn=512,
                     epilogue="residual", residual=xs, name="mix_out")
        hm = _rmsnorm(mems, norm_mem[l], BF16)
        kv = _matmul(hm, w_xkv, layer=l, out_dtype=BF16, tm=256, tn=512, name="mem_kv")
        xs, hmlp = _cross_attention(xs, norm_xattn[l], norm_mlp[l], w_xq, kv, w_xo, l)
        act = _matmul(hmlp, w_up, layer=l, out_dtype=BF16, tm=2048, tn=512, epilogue="relu2", name="mlp_up")
        dx = _matmul(act, w_down, layer=l, out_dtype=F32, tm=2048, tn=512, tk=2048, name="mlp_down")
    out = _add_rmsnorm(xs, dx, norm_final, F32, keep_sum=False)
    return out.reshape(bsz, s, d)
```

```python
import functools
import math

import numpy as np
import jax
import jax.numpy as jnp
from jax import lax
from jax.experimental import pallas as pl
from jax.experimental.pallas import tpu as pltpu

F32 = jnp.float32
BF16 = jnp.bfloat16

D_MODEL = 4096
CONV_DIM = 2048
CONV_K = 3
N_HEADS = 16
N_KV_HEADS = 4
HEAD_DIM = 128
GROUP = N_HEADS // N_KV_HEADS
ATTN_DIM = N_HEADS * HEAD_DIM
KV_DIM = N_KV_HEADS * HEAD_DIM
IDX_HEADS = 32
IDX_DIM = 64
IDX_ALL = IDX_HEADS * IDX_DIM
TOPK_MAX = 256
N_BUCKETS = 32
MAX_DISTANCE = 128
X_HEADS = 4
X_HEAD_DIM = 128
X_DIM = X_HEADS * X_HEAD_DIM
EPS = 1e-6

V7X_VMEM_BYTES = 64 * 1024 * 1024
LANES = 128
SUBLANES = 8

Q_TILE = 128
IDX_KEYS = 512
ATT_KEYS = 512
V_ROWS = HEAD_DIM + 16
BISECT_STEPS = 4
NEG_INF = float("-inf")
NT_DIMS = (((1,), (1,)), ((), ()))


INTERNAL_SCRATCH_BYTES = 8 << 20
VMEM_RESERVED_BYTES = 6 << 20


def _vmem_limit(nbytes):
    return int(min(nbytes + INTERNAL_SCRATCH_BYTES, V7X_VMEM_BYTES - VMEM_RESERVED_BYTES))


def _rmsnorm_kernel(x_ref, g_ref, o_ref):
    x = x_ref[...]
    inv = lax.rsqrt(jnp.mean(x * x, axis=-1, keepdims=True) + EPS)
    o_ref[...] = ((x * inv) * g_ref[...]).astype(o_ref.dtype)


def _rmsnorm(x, g, out_dtype, tr=256):
    rows, d = x.shape
    tr = min(tr, rows)
    return pl.pallas_call(
        _rmsnorm_kernel,
        grid=(rows // tr,),
        in_specs=[pl.BlockSpec((tr, d), lambda i: (i, 0)),
                  pl.BlockSpec((1, d), lambda i: (0, 0))],
        out_specs=pl.BlockSpec((tr, d), lambda i: (i, 0)),
        out_shape=jax.ShapeDtypeStruct((rows, d), out_dtype),
        compiler_params=pltpu.CompilerParams(
            dimension_semantics=("parallel",),
            vmem_limit_bytes=_vmem_limit(2 * tr * d * (4 + jnp.dtype(out_dtype).itemsize))),
        name="rmsnorm",
    )(x, g.reshape(1, d))


def _add_rmsnorm_kernel(x_ref, dx_ref, g_ref, *out_refs):
    x = x_ref[...] + dx_ref[...]
    inv = lax.rsqrt(jnp.mean(x * x, axis=-1, keepdims=True) + EPS)
    h_ref = out_refs[-1]
    h_ref[...] = ((x * inv) * g_ref[...]).astype(h_ref.dtype)
    if len(out_refs) == 2:
        out_refs[0][...] = x


def _add_rmsnorm(x, dx, g, out_dtype, keep_sum, tr=256):
    rows, d = x.shape
    tr = min(tr, rows)
    row_spec = pl.BlockSpec((tr, d), lambda i: (i, 0))
    out_specs = [row_spec, row_spec] if keep_sum else [row_spec]
    out_shape = [jax.ShapeDtypeStruct((rows, d), out_dtype)]
    if keep_sum:
        out_shape.insert(0, jax.ShapeDtypeStruct((rows, d), F32))
    outs = pl.pallas_call(
        _add_rmsnorm_kernel,
        grid=(rows // tr,),
        in_specs=[row_spec, row_spec, pl.BlockSpec((1, d), lambda i: (0, 0))],
        out_specs=out_specs,
        out_shape=out_shape,
        compiler_params=pltpu.CompilerParams(
            dimension_semantics=("parallel",),
            vmem_limit_bytes=_vmem_limit(2 * tr * d * (12 + jnp.dtype(out_dtype).itemsize))),
        name="add_rmsnorm",
    )(x, dx, g.reshape(1, d))
    return outs if keep_sum else outs[0]


def _mm_kernel(*refs, nk, epilogue, b_transposed):
    if epilogue == "residual":
        a_ref, b_ref, r_ref, o_ref = refs
    else:
        a_ref, b_ref, o_ref = refs
        r_ref = None

    def product():
        if b_transposed:
            return lax.dot_general(a_ref[...], b_ref[0].astype(BF16), NT_DIMS, preferred_element_type=F32)
        return jnp.dot(a_ref[...], b_ref[...].astype(BF16), preferred_element_type=F32)

    if nk > 1:
        @pl.when(pl.program_id(2) == 0)
        def _():
            o_ref[...] = jnp.zeros(o_ref.shape, o_ref.dtype)

        o_ref[...] += product()
        return
    acc = product()
    if epilogue == "relu2":
        acc = jnp.square(jnp.maximum(acc, 0.0))
    elif epilogue == "residual":
        acc = r_ref[...] + acc
    o_ref[...] = acc.astype(o_ref.dtype)


def _matmul(a, b, *, out_dtype, tm, tn, tk=None, layer=None, col0=0, n=None, b_transposed=False,
            epilogue=None, residual=None, name="matmul"):
    m, kdim = a.shape
    n_all = b.shape[-2] if b_transposed else b.shape[-1]
    n = (n_all - col0) if n is None else n
    tm, tn = min(tm, m), min(tn, n)
    tk = kdim if tk is None else tk
    nk = kdim // tk
    assert m % tm == 0 and n % tn == 0 and kdim % tk == 0
    if b_transposed:
        assert layer is not None and col0 % SUBLANES == 0
        b_spec = pl.BlockSpec((pl.Element(1), pl.Element(tn), pl.Element(tk)),
                              lambda i, j, k: (layer, pl.multiple_of(col0 + j * tn, SUBLANES), k * tk))
    else:
        assert col0 % tn == 0
        j0 = col0 // tn
        if layer is None:
            b_spec = pl.BlockSpec((tk, tn), lambda i, j, k: (k, j + j0))
        else:
            b_spec = pl.BlockSpec((None, tk, tn), lambda i, j, k: (layer, k, j + j0))
    a_buffers = 1 if (nk == 1 and tm * tk * 2 > V7X_VMEM_BYTES // 8) else 2
    in_specs = [pl.BlockSpec((tm, tk), lambda i, j, k: (i, k), pipeline_mode=pl.Buffered(a_buffers)), b_spec]
    args = [a, b]
    b_item = jnp.dtype(b.dtype).itemsize
    nbytes = a_buffers * tm * tk * 2 + 2 * tk * tn * b_item + 2 * tm * tn * jnp.dtype(out_dtype).itemsize
    if b_item != 2:
        nbytes += tk * tn * 2
    if epilogue == "residual":
        in_specs.append(pl.BlockSpec((tm, tn), lambda i, j, k: (i, j)))
        args.append(residual)
        nbytes += 2 * tm * tn * 4
    assert nk == 1 or (epilogue is None and out_dtype == F32)
    nbytes += tm * tn * 4
    return pl.pallas_call(
        functools.partial(_mm_kernel, nk=nk, epilogue=epilogue, b_transposed=b_transposed),
        grid=(m // tm, n // tn, nk),
        in_specs=in_specs,
        out_specs=pl.BlockSpec((tm, tn), lambda i, j, k: (i, j)),
        out_shape=jax.ShapeDtypeStruct((m, n), out_dtype),
        compiler_params=pltpu.CompilerParams(
            dimension_semantics=("parallel", "parallel", "arbitrary"),
            vmem_limit_bytes=_vmem_limit(nbytes)),
        name=name,
    )(*args)


def _proj_conv_kernel(h_ref, wu_ref, wcb_ref, wcc_ref, cw_ref, o_ref, wb_ref, p_ref, *, tm):
    @pl.when(pl.program_id(1) == 0)
    def _():
        wb_ref[0] = wu_ref[0].astype(BF16)
        wb_ref[1] = wcb_ref[0].astype(BF16)
        wb_ref[2] = wcc_ref[0].astype(BF16)
        p_ref[0:SUBLANES, :] = jnp.zeros((SUBLANES, p_ref.shape[1]), F32)

    h = h_ref[...]
    u = lax.dot_general(h, wb_ref[0], NT_DIMS, preferred_element_type=F32)
    cb = lax.dot_general(h, wb_ref[1], NT_DIMS, preferred_element_type=F32)
    cc = lax.dot_general(h, wb_ref[2], NT_DIMS, preferred_element_type=F32)
    p = cc * u
    p_ref[SUBLANES:SUBLANES + tm, :] = p
    p1 = p_ref[SUBLANES - 1:SUBLANES - 1 + tm, :]
    p2 = p_ref[SUBLANES - 2:SUBLANES - 2 + tm, :]
    w = cw_ref[...]
    z = w[2:3, :] * p + w[0:1, :] * p2 + w[1:2, :] * p1
    o_ref[...] = (cb * z).astype(o_ref.dtype)
    p_ref[0:SUBLANES, :] = p_ref[tm:tm + SUBLANES, :]


def _proj_gated_conv(h, w_in_t, layer, conv_w, tm=1024, tc=256):
    s, d = h.shape
    tm = min(tm, s)

    def w_spec(row0):
        return pl.BlockSpec((pl.Element(1), pl.Element(tc), pl.Element(d)),
                            lambda c, i: (layer, pl.multiple_of(row0 + c * tc, SUBLANES), 0))

    nbytes = 2 * tm * d * 2 + 3 * 2 * tc * d * 4 + 3 * tc * d * 2 + 2 * tm * tc * 2 + 6 * tm * tc * 4
    return pl.pallas_call(
        functools.partial(_proj_conv_kernel, tm=tm),
        grid=(CONV_DIM // tc, s // tm),
        in_specs=[pl.BlockSpec((tm, d), lambda c, i: (i, 0)),
                  w_spec(0), w_spec(CONV_DIM), w_spec(2 * CONV_DIM),
                  pl.BlockSpec((CONV_K, tc), lambda c, i: (0, c))],
        out_specs=pl.BlockSpec((tm, tc), lambda c, i: (i, c)),
        out_shape=jax.ShapeDtypeStruct((s, CONV_DIM), BF16),
        scratch_shapes=[pltpu.VMEM((3, tc, d), BF16), pltpu.VMEM((tm + SUBLANES, tc), F32)],
        compiler_params=pltpu.CompilerParams(
            dimension_semantics=("parallel", "arbitrary"),
            vmem_limit_bytes=_vmem_limit(nbytes)),
        name="proj_conv",
    )(h, w_in_t, w_in_t, w_in_t, conv_w)


def _t5_bucket_np(dist):
    n = np.maximum(dist, 0)
    max_exact = N_BUCKETS // 2
    nf = np.maximum(n, 1).astype(np.float64)
    large = max_exact + (np.log(nf / max_exact) / math.log(MAX_DISTANCE / max_exact)
                         * (N_BUCKETS - max_exact)).astype(np.int32)
    large = np.minimum(large, N_BUCKETS - 1)
    return np.where(n < max_exact, n, large).astype(np.int32)


def _near_bias(rel_bias):
    sl = np.arange(Q_TILE)[:, None]
    tl = np.arange(Q_TILE)[None, :]
    dist = np.stack([tl - sl + Q_TILE, tl - sl])
    assert _t5_bucket_np(np.array([Q_TILE]))[0] == N_BUCKETS - 1
    bucket = _t5_bucket_np(dist)
    rel = (rel_bias - rel_bias[N_BUCKETS - 1][None, :]) * (HEAD_DIM ** 0.5)
    onehot = np.eye(N_BUCKETS, dtype=np.float32)[bucket.reshape(-1)]
    b = jnp.dot(onehot, rel, precision=lax.Precision.HIGHEST)
    b = b.reshape(2, Q_TILE, Q_TILE, N_KV_HEADS, GROUP)
    b = b.transpose(0, 3, 1, 4, 2)
    b = b.reshape(2, N_KV_HEADS, Q_TILE, GROUP * Q_TILE).astype(F32)
    return jnp.concatenate([jnp.zeros_like(b[:1]), b], axis=0)


def _dsa_kernel(q_ref, k_ref, iq_lo_ref, iq_hi_ref, vt_ref, ik2_ref, w_ref, bias_ref, o_ref,
                sc_ref, s_ref, acc_ref, m_ref, *, topk, max_iters, pos_bits):
    i = pl.program_id(0)
    q0 = i * Q_TILE
    kf = float(topk)
    per = ATT_KEYS // Q_TILE
    c_last = i // per
    n_cnt = c_last + 1
    n_idx = (q0 + Q_TILE + IDX_KEYS - 1) // IDX_KEYS
    qpos = q0 + lax.broadcasted_iota(jnp.int32, (1, Q_TILE), 1)

    def fold8(x, op):
        rows = x.shape[0]
        if rows > SUBLANES * SUBLANES:
            x = op(x.reshape(SUBLANES, rows // SUBLANES, x.shape[1]), axis=0)
        return op(x.reshape(x.shape[0] // SUBLANES, SUBLANES, x.shape[1]), axis=0)

    def col_reduce(x, op):
        return op(fold8(x, op), axis=0, keepdims=True)

    n_dots = IDX_ALL // (2 * LANES)
    half = n_dots // 2

    def head_pairs(d):
        ref, e = (iq_lo_ref, d) if d < half else (iq_hi_ref, d - half)
        return jnp.concatenate([ref[:, (2 * e) * LANES:(2 * e + 1) * LANES],
                                ref[:, (2 * e + 1) * LANES:(2 * e + 2) * LANES]], axis=0)

    rhs = [head_pairs(d) for d in range(n_dots)]

    def idx_body(j, carry):
        mx, mn = carry
        s0 = pl.multiple_of(j * IDX_KEYS, IDX_KEYS)
        lhs = jnp.concatenate([ik2_ref[pl.ds(s0, IDX_KEYS), 0:LANES],
                               ik2_ref[pl.ds(s0, IDX_KEYS), LANES:2 * LANES]], axis=0)
        acc = None
        for d in range(n_dots):
            r = lax.dot_general(lhs, rhs[d], NT_DIMS, preferred_element_type=F32)
            r = jnp.maximum(r, 0.0)
            t = r[:IDX_KEYS] * w_ref[2 * d:2 * d + 1, :] + r[IDX_KEYS:] * w_ref[2 * d + 1:2 * d + 2, :]
            acc = t if acc is None else acc + t
        acc = acc[:, :Q_TILE] + acc[:, Q_TILE:]
        kpos = s0 + lax.broadcasted_iota(jnp.int32, (IDX_KEYS, 1), 0)
        causal = kpos <= qpos
        lo_fill = jnp.where(causal, acc, NEG_INF)
        sc_ref[pl.ds(s0, IDX_KEYS), :] = lo_fill
        mx = jnp.maximum(mx, fold8(lo_fill, jnp.max))
        mn = jnp.minimum(mn, fold8(jnp.where(causal, acc, -NEG_INF), jnp.min))
        return mx, mn

    def idx_pair(jj, carry):
        return idx_body(2 * jj + 1, idx_body(2 * jj, carry))

    n_pairs = n_idx // 2
    carry = lax.fori_loop(
        0, n_pairs, idx_pair,
        (jnp.full((SUBLANES, Q_TILE), NEG_INF, F32), jnp.full((SUBLANES, Q_TILE), -NEG_INF, F32)))
    mx8, mn8 = lax.fori_loop(2 * n_pairs, n_idx, idx_body, carry)
    hi0 = jnp.max(mx8, axis=0, keepdims=True)
    lo0 = jnp.min(mn8, axis=0, keepdims=True)

    def fill_body(j, carry):
        sc_ref[pl.ds(pl.multiple_of(j * IDX_KEYS, IDX_KEYS), IDX_KEYS), :] = jnp.full(
            (IDX_KEYS, Q_TILE), NEG_INF, F32)
        return carry

    lax.fori_loop(n_idx, n_cnt * (ATT_KEYS // IDX_KEYS), fill_body, 0)

    def key_pos(c):
        return c * ATT_KEYS + lax.broadcasted_iota(jnp.int32, (ATT_KEYS, 1), 0)

    def count_rows(pred):
        def body(c, cnt):
            blk = sc_ref[pl.ds(pl.multiple_of(c * ATT_KEYS, ATT_KEYS), ATT_KEYS), :]
            return cnt + fold8(pred(blk, key_pos(c)), jnp.sum)
        c8 = lax.fori_loop(0, n_cnt, body, jnp.zeros((SUBLANES, Q_TILE), F32))
        return jnp.sum(c8, axis=0, keepdims=True)

    def count_ge(tau):
        slabs = SUBLANES
        rows = ATT_KEYS // slabs

        def body(c, cnt):
            blk = sc_ref[pl.ds(pl.multiple_of(c * ATT_KEYS, ATT_KEYS), ATT_KEYS), :]
            for j in range(slabs):
                cnt = jnp.where(blk[j * rows:(j + 1) * rows] >= tau, cnt + 1.0, cnt)
            return cnt

        cnt = lax.fori_loop(0, n_cnt, body, jnp.zeros((rows, Q_TILE), F32))
        return jnp.sum(fold8(cnt, jnp.sum), axis=0, keepdims=True)

    def n_active(lo, hi, flo):
        mid = 0.5 * lo + 0.5 * hi
        act = jnp.logical_and(flo > kf, jnp.logical_and(mid > lo, mid < hi))
        return act, jnp.sum(jnp.where(act, 1.0, 0.0))

    flo0 = (qpos + 1).astype(F32)

    def bis_cond(st):
        _, _, _, n, it = st
        return jnp.logical_and(n > 0.0, it < max_iters)

    def bis_body(st):
        lo, hi, flo, _, it = st
        for _ in range(BISECT_STEPS):
            act, _ = n_active(lo, hi, flo)
            mid = 0.5 * lo + 0.5 * hi
            c = count_ge(mid)
            up = jnp.logical_and(act, c >= kf)
            dn = jnp.logical_and(act, c < kf)
            lo = jnp.where(up, mid, lo)
            flo = jnp.where(up, c, flo)
            hi = jnp.where(dn, mid, hi)
        _, n = n_active(lo, hi, flo)
        return lo, hi, flo, n, it + 1

    _, n0 = n_active(lo0, hi0, flo0)
    lo, hi, flo, _, _ = lax.while_loop(bis_cond, bis_body, (lo0, hi0, flo0, n0, jnp.int32(0)))
    open_hi = jnp.logical_and(flo > kf, hi == hi0)
    n_open = jnp.sum(jnp.where(open_hi, 1.0, 0.0))
    c_hi = lax.cond(n_open > 0.0, lambda: count_ge(hi), lambda: jnp.zeros((1, Q_TILE), F32))
    use_hi = jnp.logical_and(open_hi, c_hi >= kf)
    tau = jnp.where(use_hi, hi, lo)
    n_ge = jnp.where(use_hi, c_hi, flo)

    tied = n_ge > kf
    n_keys = n_cnt * ATT_KEYS

    def tie_cut():
        n_gt = count_rows(lambda blk, pos: jnp.where(blk > tau, 1.0, 0.0))
        need = kf - n_gt

        def step(_, st):
            lo_p, hi_p = st
            mid = lax.shift_right_logical(lo_p + hi_p, 1)
            below = count_rows(
                lambda blk, pos: jnp.where(blk == tau, jnp.where(pos < mid, 1.0, 0.0), 0.0))
            ge = below >= need
            return jnp.where(ge, lo_p, mid), jnp.where(ge, mid, hi_p)

        lo_p = jnp.zeros((1, Q_TILE), jnp.int32)
        hi_p = jnp.zeros((1, Q_TILE), jnp.int32) + n_keys
        _, hi_p = lax.fori_loop(0, pos_bits, step, (lo_p, hi_p))
        return hi_p

    n_tied = jnp.sum(jnp.where(tied, 1.0, 0.0))
    cut = lax.cond(n_tied > 0.0, tie_cut, lambda: jnp.zeros((1, Q_TILE), jnp.int32) + n_keys)
    cut = jnp.where(tied, cut, n_keys)

    def mask_body(c, carry):
        rows = pl.ds(pl.multiple_of(c * ATT_KEYS, ATT_KEYS), ATT_KEYS)
        blk = sc_ref[rows, :]
        at_tau = jnp.where(key_pos(c) < cut, 0.0, NEG_INF)
        sc_ref[rows, :] = jnp.where(blk > tau, 0.0, jnp.where(blk == tau, at_tau, NEG_INF))
        return carry

    lax.fori_loop(0, n_cnt, mask_body, 0)

    c2 = (HEAD_DIM ** -0.5) * math.log2(math.e)
    qg = []
    for g in range(N_KV_HEADS):
        qg.append(jnp.concatenate(
            [q_ref[:, (g * GROUP + hh) * HEAD_DIM:(g * GROUP + hh + 1) * HEAD_DIM] for hh in range(GROUP)],
            axis=0))

    m_ref[...] = jnp.full(m_ref.shape, NEG_INF, F32)
    acc_ref[...] = jnp.zeros(acc_ref.shape, F32)

    def logits(c, g, slot):
        s0 = pl.multiple_of(jnp.minimum(c, n_cnt - 1) * ATT_KEYS, ATT_KEYS)
        kb = k_ref[pl.ds(s0, ATT_KEYS), g * HEAD_DIM:(g + 1) * HEAD_DIM]
        s_ref[slot] = lax.dot_general(kb, qg[g], NT_DIMS, preferred_element_type=F32)

    def chunk(c, near):
        s0 = pl.multiple_of(c * ATT_KEYS, ATT_KEYS)
        madd = jnp.concatenate([sc_ref[pl.ds(s0, ATT_KEYS), :]] * GROUP, axis=1)
        for g in range(N_KV_HEADS):
            if g + 1 < N_KV_HEADS:
                logits(c, g + 1, (g + 1) % 2)
            else:
                logits(c + 1, 0, 0)
            vtb = vt_ref[g, :, pl.ds(s0, ATT_KEYS)]
            s = s_ref[g % 2] + madd
            if near:
                rows = []
                for r in range(per):
                    d = i - (c * per + r)
                    rows.append(bias_ref[jnp.where(d == 1, 1, jnp.where(d == 0, 2, 0)), g])
                s = s + jnp.concatenate(rows, axis=0)
            m_old = m_ref[g]
            m_new = jnp.maximum(m_old, col_reduce(s, jnp.max))
            m_use = jnp.where(m_new == NEG_INF, 0.0, m_new)
            alpha = jnp.exp2(c2 * (m_old - m_use))
            p = jnp.exp2(c2 * (s - m_use))
            pv = jnp.dot(vtb, p.astype(BF16), preferred_element_type=F32)
            acc_ref[g] = alpha * acc_ref[g] + pv
            m_ref[g] = m_new

    n_far = jnp.where(i % per == 0, jnp.maximum(c_last - 1, 0), c_last)
    logits(0, 0, 0)

    def far_body(c, carry):
        chunk(c, False)
        return carry

    def far_pair(cc, carry):
        chunk(2 * cc, False)
        chunk(2 * cc + 1, False)
        return carry

    n_far_pairs = n_far // 2
    lax.fori_loop(0, n_far_pairs, far_pair, 0)
    lax.fori_loop(2 * n_far_pairs, n_far, far_body, 0)

    def near_body(c, carry):
        chunk(c, True)
        return carry

    lax.fori_loop(n_far, n_cnt, near_body, 0)

    for g in range(N_KV_HEADS):
        out = acc_ref[g, 0:HEAD_DIM, :] / acc_ref[g, HEAD_DIM:HEAD_DIM + 1, :]
        for hh in range(GROUP):
            h = g * GROUP + hh
            o_ref[:, h * HEAD_DIM:(h + 1) * HEAD_DIM] = (
                out[:, hh * Q_TILE:(hh + 1) * Q_TILE].T.astype(o_ref.dtype))


def _dsa_attention(zb, zc, rel_bias, topk):
    s = zb.shape[0]
    nq = s // Q_TILE
    assert s % ATT_KEYS == 0
    k_col = ATTN_DIM // KV_DIM
    iq_half = IDX_ALL // 2
    iq_col = (ATTN_DIM + 2 * KV_DIM) // iq_half
    assert (ATTN_DIM + 2 * KV_DIM) % iq_half == 0
    vt = zb[:, ATTN_DIM + KV_DIM:ATTN_DIM + 2 * KV_DIM].T.reshape(N_KV_HEADS, HEAD_DIM, s)
    vt = jnp.concatenate([vt, jnp.ones((N_KV_HEADS, V_ROWS - HEAD_DIM, s), BF16)], axis=1)
    ik = zc[:, :IDX_DIM].astype(BF16)
    ik2 = jnp.concatenate([ik, jnp.zeros((s, 2 * IDX_DIM), BF16), ik], axis=1)
    w = zc[:, IDX_DIM:IDX_DIM + IDX_HEADS] * (IDX_HEADS ** -0.5) * (IDX_DIM ** -0.5)
    w = w.reshape(nq, Q_TILE, IDX_HEADS // 4, 2, 2).transpose(0, 2, 4, 3, 1)
    w = w.reshape(nq, IDX_HEADS // 2, 2 * Q_TILE)
    bias = _near_bias(rel_bias)

    resident = dict(pipeline_mode=pl.Buffered(1))
    nbytes = (s * KV_DIM * 2 + vt.size * 2 + s * 2 * LANES * 2 + bias.size * 4 + s * Q_TILE * 4
              + 2 * (2 * Q_TILE * ATTN_DIM * 2 * 2 + IDX_HEADS * Q_TILE * 4)
              + 6 * ATT_KEYS * GROUP * Q_TILE * 4 + 3 * N_KV_HEADS * V_ROWS * GROUP * Q_TILE * 4)
    return pl.pallas_call(
        functools.partial(_dsa_kernel, topk=topk, max_iters=400, pos_bits=int(s).bit_length()),
        grid=(nq,),
        in_specs=[pl.BlockSpec((Q_TILE, ATTN_DIM), lambda i: (i, 0)),
                  pl.BlockSpec((s, KV_DIM), lambda i: (0, k_col), **resident),
                  pl.BlockSpec((Q_TILE, iq_half), lambda i: (i, iq_col)),
                  pl.BlockSpec((Q_TILE, iq_half), lambda i: (i, iq_col + 1)),
                  pl.BlockSpec(vt.shape, lambda i: (0, 0, 0), **resident),
                  pl.BlockSpec((s, 2 * LANES), lambda i: (0, 0), **resident),
                  pl.BlockSpec((None, IDX_HEADS // 2, 2 * Q_TILE), lambda i: (i, 0, 0)),
                  pl.BlockSpec(bias.shape, lambda i: (0, 0, 0, 0), **resident)],
        out_specs=pl.BlockSpec((Q_TILE, ATTN_DIM), lambda i: (i, 0)),
        out_shape=jax.ShapeDtypeStruct((s, ATTN_DIM), BF16),
        scratch_shapes=[pltpu.VMEM((s, Q_TILE), F32),
                        pltpu.VMEM((2, ATT_KEYS, GROUP * Q_TILE), F32),
                        pltpu.VMEM((N_KV_HEADS, V_ROWS, GROUP * Q_TILE), F32),
                        pltpu.VMEM((N_KV_HEADS, 1, GROUP * Q_TILE), F32)],
        compiler_params=pltpu.CompilerParams(
            dimension_semantics=("arbitrary",),
            vmem_limit_bytes=_vmem_limit(nbytes)),
        name="dsa_attention",
    )(zb, zb, zb, zb, vt, ik2, w, bias)


def _merge_kernel(a_ref, o_ref, wa_ref, wo_ref, ga_ref, gb_ref, out_ref, wab_ref, wob_ref):
    @pl.when(pl.program_id(1) == 0)
    def _():
        wab_ref[...] = wa_ref[...].astype(BF16)
        wob_ref[...] = wo_ref[...].astype(BF16)

    ya = jnp.dot(a_ref[...], wab_ref[...], preferred_element_type=F32)
    yb = jnp.dot(o_ref[...], wob_ref[...], preferred_element_type=F32)
    merged = jax.nn.sigmoid(ga_ref[...]) * ya + jax.nn.sigmoid(gb_ref[...]) * yb
    out_ref[...] = merged.astype(out_ref.dtype)


def _merge(a_in, o, w_conv_out, w_attn_out, layer, zg, g_col0, tm=512, tn=1024):
    s = a_in.shape[0]
    tm = min(tm, s)
    nn = D_MODEL // tn
    assert g_col0 % tn == 0
    gj = g_col0 // tn
    once = dict(pipeline_mode=pl.Buffered(1))
    nbytes = (2 * (2 * tm * CONV_DIM * 2 + 2 * tm * tn * 4 + tm * tn * 2) + 2 * CONV_DIM * tn * 4
              + 2 * CONV_DIM * tn * 2 + 4 * tm * tn * 4)
    return pl.pallas_call(
        _merge_kernel,
        grid=(nn, s // tm),
        in_specs=[pl.BlockSpec((tm, CONV_DIM), lambda j, i: (i, 0)),
                  pl.BlockSpec((tm, ATTN_DIM), lambda j, i: (i, 0)),
                  pl.BlockSpec((None, CONV_DIM, tn), lambda j, i: (layer, 0, j), **once),
                  pl.BlockSpec((None, ATTN_DIM, tn), lambda j, i: (layer, 0, j), **once),
                  pl.BlockSpec((tm, tn), lambda j, i: (i, j + gj)),
                  pl.BlockSpec((tm, tn), lambda j, i: (i, j + gj + nn))],
        out_specs=pl.BlockSpec((tm, tn), lambda j, i: (i, j)),
        out_shape=jax.ShapeDtypeStruct((s, D_MODEL), BF16),
        scratch_shapes=[pltpu.VMEM((CONV_DIM, tn), BF16), pltpu.VMEM((ATTN_DIM, tn), BF16)],
        compiler_params=pltpu.CompilerParams(
            dimension_semantics=("parallel", "arbitrary"),
            vmem_limit_bytes=_vmem_limit(nbytes)),
        name="merge",
    )(a_in, o, w_conv_out, w_attn_out, zg, zg)


def _xattn_kernel(x_ref, gx_ref, gm_ref, wq_ref, kx_ref, vx_ref, wo_ref, out_ref, h_ref, wqb_ref, wob_ref):
    @pl.when(pl.program_id(0) == 0)
    def _():
        wqb_ref[...] = wq_ref[...].astype(BF16)
        wob_ref[...] = wo_ref[...].astype(BF16)

    def rms(v, g_ref):
        inv = lax.rsqrt(jnp.mean(v * v, axis=-1, keepdims=True) + EPS)
        return ((v * inv) * g_ref[...]).astype(BF16)

    x = x_ref[...]
    qx = jnp.dot(rms(x, gx_ref), wqb_ref[...], preferred_element_type=F32).astype(BF16)
    scale = X_HEAD_DIM ** -0.5
    outs = []
    for h in range(X_HEADS):
        sl = slice(h * X_HEAD_DIM, (h + 1) * X_HEAD_DIM)
        s = lax.dot_general(qx[:, sl], kx_ref[:, sl], NT_DIMS, preferred_element_type=F32) * scale
        m = jnp.max(s, axis=-1, keepdims=True)
        p = jnp.exp(s - m)
        l = jnp.sum(p, axis=-1, keepdims=True)
        oh = jnp.dot(p.astype(BF16), vx_ref[:, sl], preferred_element_type=F32) / l
        outs.append(oh.astype(BF16))
    o = jnp.concatenate(outs, axis=1)
    x_new = x + jnp.dot(o, wob_ref[...], preferred_element_type=F32)
    out_ref[...] = x_new
    h_ref[...] = rms(x_new, gm_ref)


def _cross_attention(x, g_xattn, g_mlp, w_xq, kv, w_xo, layer, tm=256):
    s = x.shape[0]
    tm = min(tm, s)
    n_mem = kv.shape[0]
    resident = dict(pipeline_mode=pl.Buffered(1))
    nbytes = (2 * (2 * tm * D_MODEL * 4 + tm * D_MODEL * 2) + 2 * D_MODEL * X_DIM * (4 + 2)
              + 2 * n_mem * X_DIM * 2 + 4 * tm * D_MODEL * 4)
    return pl.pallas_call(
        _xattn_kernel,
        grid=(s // tm,),
        in_specs=[pl.BlockSpec((tm, D_MODEL), lambda i: (i, 0)),
                  pl.BlockSpec((1, D_MODEL), lambda i: (0, 0)),
                  pl.BlockSpec((1, D_MODEL), lambda i: (0, 0)),
                  pl.BlockSpec((None, D_MODEL, X_DIM), lambda i: (layer, 0, 0), **resident),
                  pl.BlockSpec((n_mem, X_DIM), lambda i: (0, 0), **resident),
                  pl.BlockSpec((n_mem, X_DIM), lambda i: (0, 1), **resident),
                  pl.BlockSpec((None, X_DIM, D_MODEL), lambda i: (layer, 0, 0), **resident)],
        out_specs=[pl.BlockSpec((tm, D_MODEL), lambda i: (i, 0)),
                   pl.BlockSpec((tm, D_MODEL), lambda i: (i, 0))],
        out_shape=[jax.ShapeDtypeStruct((s, D_MODEL), F32),
                   jax.ShapeDtypeStruct((s, D_MODEL), BF16)],
        scratch_shapes=[pltpu.VMEM((D_MODEL, X_DIM), BF16), pltpu.VMEM((X_DIM, D_MODEL), BF16)],
        compiler_params=pltpu.CompilerParams(
            dimension_semantics=("arbitrary",),
            vmem_limit_bytes=_vmem_limit(nbytes)),
        name="cross_attention",
    )(x, g_xattn.reshape(1, D_MODEL), g_mlp.reshape(1, D_MODEL), w_xq, kv, kv, w_xo)


CONV_END = 3 * CONV_DIM
ATTN_END = CONV_END + ATTN_DIM + 2 * KV_DIM + IDX_ALL
IW_END = ATTN_END + IDX_DIM + IDX_HEADS


def kernel(x, mem, rel_bias, norm_mix, w_in, conv_w, w_conv_out, w_attn_out, w_mix_out, norm_xattn, norm_mem, w_xq, w_xkv, w_xo, norm_mlp, w_up, w_down, norm_final):
    bsz, s, d = x.shape
    assert bsz == 1 and d == D_MODEL
    depth = w_in.shape[0]
    topk = min(TOPK_MAX, s // 4)
    xs = x.reshape(s, d)
    mems = mem.reshape(mem.shape[1], d)
    w_in_t = jnp.swapaxes(w_in, 1, 2)
    dx = None
    for l in range(depth):
        if dx is None:
            h = _rmsnorm(xs, norm_mix[l], BF16)
        else:
            xs, h = _add_rmsnorm(xs, dx, norm_mix[l], BF16, keep_sum=True)
        proj = functools.partial(_matmul, h, w_in_t, layer=l, b_transposed=True, tm=2048)
        a_in = _proj_gated_conv(h, w_in_t, l, conv_w[l])
        zb = proj(col0=CONV_END, n=ATTN_END - CONV_END, out_dtype=BF16, tn=512, name="proj_attn")
        zc = _matmul(h, w_in_t, layer=l, b_transposed=True, col0=ATTN_END, n=LANES, out_dtype=F32,
                     tm=1024, tn=LANES, name="proj_idx")
        zg = proj(col0=IW_END, n=2 * D_MODEL, out_dtype=F32, tn=512, name="proj_gate")
        o = _dsa_attention(zb, zc, rel_bias, topk)
        merged = _merge(a_in, o, w_conv_out, w_attn_out, l, zg, 0)
        xs = _matmul(merged, w_mix_out, layer=l, out_dtype=F32, tm=1024, tn=512,
                     epilogue="residual", residual=xs, name="mix_out")
        hm = _rmsnorm(mems, norm_mem[l], BF16)
        kv = _matmul(hm, w_xkv, layer=l, out_dtype=BF16, tm=256, tn=512, name="mem_kv")
        xs, hmlp = _cross_attention(xs, norm_xattn[l], norm_mlp[l], w_xq, kv, w_xo, l)
        act = _matmul(hmlp, w_up, layer=l, out_dtype=BF16, tm=2048, tn=512, epilogue="relu2", name="mlp_up")
        dx = _matmul(act, w_down, layer=l, out_dtype=F32, tm=2048, tn=1024, tk=1024, name="mlp_down")
    out = _add_rmsnorm(xs, dx, norm_final, F32, keep_sum=False)
    return out.reshape(bsz, s, d)
```
